```python
import math
import jax, jax.numpy as jnp
from jax import lax
import numpy as np

D_MODEL = 1024
BATCH = 8
SEQ = 2048
DEPTH = 1
DEC_BATCH = 128
DEC_SEQ = 4
PAST_LEN = 16384
PAGE_SIZE = 128

PLE_DIM = 256
D_A = D_MODEL
D_B = D_MODEL
SSD_HEAD_DIM = 64
SSD_HEADS = D_A // SSD_HEAD_DIM
SSD_STATE = 128
SSD_GROUPS = 2
SSD_HPG = SSD_HEADS // SSD_GROUPS
SSD_CHUNK = 128
CONV_K = 4
CONV_DIM = D_A + 2 * SSD_GROUPS * SSD_STATE
S5_CH = 16
S5_GROUPS = D_B // S5_CH
S5_STATE = 64
D_IN = D_A + CONV_DIM + SSD_HEADS + D_B + D_B
EPS = 1e-6
DT_MIN = 1e-3
DT_MAX = 1e-1

kernel_name = 'hymba_ssd_s5_decode_step'


def rmsnorm(x, g):
    xf = x.astype(jnp.float32)
    y = xf * lax.rsqrt(jnp.mean(xf * xf, axis=-1, keepdims=True) + EPS)
    return (y * g.astype(jnp.float32)).astype(x.dtype)


def cmul(ar, ai, br, bi):
    return ar * br - ai * bi, ar * bi + ai * br


def causal_conv(xbc, buf, w, b):
    full = jnp.concatenate([buf.astype(xbc.dtype), xbc], axis=1)
    y = lax.conv_general_dilated(full, w[:, None, :].astype(full.dtype), window_strides=(1,),
                                 padding='VALID', dimension_numbers=('NWC', 'WIO', 'NWC'),
                                 feature_group_count=full.shape[-1])
    return jax.nn.silu(y + b.astype(y.dtype)), full[:, -(CONV_K - 1):, :]


def ssd_scan(x, dt, A, Bm, Cm, h0):
    b, T = x.shape[:2]
    L = math.gcd(T, SSD_CHUNK)
    nc = T // L
    x = x.reshape((b, nc, L) + x.shape[2:])
    dt = dt.reshape((b, nc, L) + dt.shape[2:])
    Bm = Bm.reshape((b, nc, L) + Bm.shape[2:])
    Cm = Cm.reshape((b, nc, L) + Cm.shape[2:])
    Acs = jnp.cumsum(dt * A, axis=2)
    Acs_t = jnp.transpose(Acs, (0, 1, 3, 4, 2))
    seg = Acs_t[..., :, None] - Acs_t[..., None, :]
    causal = jnp.tril(jnp.ones((L, L), dtype=bool))
    decay = jnp.exp(jnp.where(causal, seg, -jnp.inf))
    dt_t = jnp.transpose(dt, (0, 1, 3, 4, 2))
    CB = jnp.einsum('bclgn,bcsgn->bcgls', Cm, Bm)
    M = CB[:, :, :, None] * decay * dt_t[..., None, :]
    y_diag = jnp.einsum('bcgels,bcsgep->bclgep', M, x)
    w_end = jnp.exp(Acs[:, :, -1:] - Acs) * dt
    states = jnp.einsum('bclgn,bclge,bclgep->bcgepn', Bm, w_end, x)
    chunk_decay = jnp.exp(Acs[:, :, -1])

    def step(h, inp):
        s, d = inp
        return d[..., None, None] * h + s, h

    h_final, h_starts = lax.scan(step, h0, (jnp.moveaxis(states, 1, 0), jnp.moveaxis(chunk_decay, 1, 0)))
    h_starts = jnp.moveaxis(h_starts, 0, 1)
    y_off = jnp.einsum('bclgn,bcgepn,bclge->bclgep', Cm, h_starts, jnp.exp(Acs))
    y = (y_diag + y_off).reshape((b, T) + x.shape[3:])
    return y, h_final


def ssd_branch(z, xbc, dt_raw, conv_buf, h0, conv_w, conv_b, dt_bias, a_log, ssd_d, ssd_norm_g):
    f32 = jnp.float32
    b, T, _ = xbc.shape
    xbc, new_buf = causal_conv(xbc, conv_buf, conv_w, conv_b)
    xbc = xbc.astype(f32)
    xs, Bm, Cm = jnp.split(xbc, [D_A, D_A + SSD_GROUPS * SSD_STATE], axis=-1)
    xs = xs.reshape(b, T, SSD_GROUPS, SSD_HPG, SSD_HEAD_DIM)
    Bm = Bm.reshape(b, T, SSD_GROUPS, SSD_STATE)
    Cm = Cm.reshape(b, T, SSD_GROUPS, SSD_STATE)
    dt = jax.nn.softplus(dt_raw.astype(f32) + dt_bias.astype(f32)).reshape(b, T, SSD_GROUPS, SSD_HPG)
    A = -jnp.exp(a_log.astype(f32)).reshape(SSD_GROUPS, SSD_HPG)
    h0 = h0.astype(f32).reshape(b, SSD_GROUPS, SSD_HPG, SSD_HEAD_DIM, SSD_STATE)
    y, hT = ssd_scan(xs, dt, A, Bm, Cm, h0)
    y = y + ssd_d.astype(f32).reshape(SSD_GROUPS, SSD_HPG)[..., None] * xs
    gw = SSD_HPG * SSD_HEAD_DIM
    y = y.reshape(b, T, SSD_GROUPS, gw) * jax.nn.silu(z.astype(f32)).reshape(b, T, SSD_GROUPS, gw)
    y = y * lax.rsqrt(jnp.mean(y * y, axis=-1, keepdims=True) + EPS) * ssd_norm_g.astype(f32).reshape(SSD_GROUPS, gw)
    return y.reshape(b, T, D_A), new_buf, hT.reshape(b, SSD_HEADS, SSD_HEAD_DIM, SSD_STATE)


def s5_branch(u, z, h0_re, h0_im, lam_re, lam_im, log_dt, b_re, b_im, c_re, c_im, s5_d, glu_w, glu_b):
    f32 = jnp.float32
    b, T, _ = u.shape
    uf = u.astype(f32).reshape(b, T, S5_GROUPS, S5_CH)
    lr = lam_re.astype(f32)
    li = lam_im.astype(f32)
    delta = jnp.exp(log_dt.astype(f32))[:, None]
    mag = jnp.exp(lr * delta)
    ab_re = mag * jnp.cos(li * delta)
    ab_im = mag * jnp.sin(li * delta)
    q_re = ab_re - 1.0
    den = lr * lr + li * li
    f_re = (q_re * lr + ab_im * li) / den
    f_im = (ab_im * lr - q_re * li) / den
    bb_re, bb_im = cmul(f_re[..., None], f_im[..., None], b_re.astype(f32), b_im.astype(f32))
    bu_re = jnp.einsum('btgh,gph->btgp', uf, bb_re)
    bu_im = jnp.einsum('btgh,gph->btgp', uf, bb_im)
    a_re = jnp.broadcast_to(ab_re, bu_re.shape)
    a_im = jnp.broadcast_to(ab_im, bu_im.shape)

    def combine(l, r):
        a1r, a1i, b1r, b1i = l
        a2r, a2i, b2r, b2i = r
        ar, ai = cmul(a2r, a2i, a1r, a1i)
        br, bi = cmul(a2r, a2i, b1r, b1i)
        return ar, ai, br + b2r, bi + b2i

    acr, aci, xr, xi = lax.associative_scan(combine, (a_re, a_im, bu_re, bu_im), axis=1)
    pr, pi = cmul(acr, aci, h0_re.astype(f32)[:, None], h0_im.astype(f32)[:, None])
    xr = xr + pr
    xi = xi + pi
    y = (jnp.einsum('btgp,ghp->btgh', xr, c_re.astype(f32))
         - jnp.einsum('btgp,ghp->btgh', xi, c_im.astype(f32))
         + s5_d.astype(f32) * uf).reshape(b, T, D_B)
    g = jax.nn.gelu(y)
    y = g * jax.nn.sigmoid(g @ glu_w.astype(f32) + glu_b.astype(f32))
    y = y * jax.nn.silu(z.astype(f32))
    return y, xr[:, -1], xi[:, -1]


def trunk_layer(h, p, conv_buf, ssd_h, s5_re, s5_im,
                w_in, g_in, conv_w, conv_b, dt_bias, a_log, ssd_d, ssd_norm_g,
                s5_lambda_re, s5_lambda_im, s5_log_dt, s5_b_re, s5_b_im, s5_c_re, s5_c_im, s5_d,
                glu_w, glu_b, w_out, g_ple, w_ple_gate, w_ple_proj):
    n = rmsnorm(h, g_in)
    proj = n @ w_in
    i1 = D_A
    i2 = i1 + CONV_DIM
    i3 = i2 + SSD_HEADS
    i4 = i3 + D_B
    z_a, xbc, dt_raw, z_b, u_b = jnp.split(proj, [i1, i2, i3, i4], axis=-1)
    ya, conv_new, ssd_new = ssd_branch(z_a, xbc, dt_raw, conv_buf, ssd_h, conv_w, conv_b,
                                       dt_bias, a_log, ssd_d, ssd_norm_g)
    yb, s5r_new, s5i_new = s5_branch(u_b, z_b, s5_re, s5_im, s5_lambda_re, s5_lambda_im, s5_log_dt,
                                     s5_b_re, s5_b_im, s5_c_re, s5_c_im, s5_d, glu_w, glu_b)
    mix = jnp.concatenate([ya, yb], axis=-1).astype(h.dtype) @ w_out
    h = h + mix
    gate = jax.nn.sigmoid(rmsnorm(h, g_ple) @ w_ple_gate)
    h = h + (p @ w_ple_proj) * gate
    return h, conv_new, ssd_new, s5r_new, s5i_new


def setup_inputs(seed: int = 0) -> dict:
    key = jax.random.key(seed)
    ks = jax.random.split(key, 40)
    f32 = jnp.float32
    nrm = lambda k, s, sc: jax.random.normal(k, s, f32) * sc
    dt0 = jnp.exp(jax.random.uniform(ks[10], (DEPTH, SSD_HEADS), f32, math.log(DT_MIN), math.log(DT_MAX)))
    lam_im0 = jnp.pi * jnp.arange(S5_STATE, dtype=f32)
    return {
        'x_prompt': nrm(ks[0], (BATCH, SEQ, D_MODEL), 1.0),
        'x_sample': nrm(ks[1], (DEC_BATCH, DEC_SEQ, D_MODEL), 1.0),
        'p_prompt': nrm(ks[2], (DEPTH, BATCH, SEQ, PLE_DIM), 1.0),
        'p_sample': nrm(ks[3], (DEPTH, DEC_BATCH, DEC_SEQ, PLE_DIM), 1.0),
        'state_ssd': nrm(ks[4], (DEPTH, DEC_BATCH, SSD_HEADS, SSD_HEAD_DIM, SSD_STATE), 0.1),
        'state_conv': nrm(ks[5], (DEPTH, DEC_BATCH, CONV_K - 1, CONV_DIM), 1.0),
        'state_s5_re': nrm(ks[6], (DEPTH, DEC_BATCH, S5_GROUPS, S5_STATE), 0.1),
        'state_s5_im': nrm(ks[7], (DEPTH, DEC_BATCH, S5_GROUPS, S5_STATE), 0.1),
        'w_in': nrm(ks[8], (DEPTH, D_MODEL, D_IN), D_MODEL ** -0.5),
        'g_in': 1.0 + nrm(ks[9], (DEPTH, D_MODEL), 0.01),
        'conv_w': nrm(ks[11], (DEPTH, CONV_K, CONV_DIM), 0.5 * CONV_K ** -0.5),
        'conv_b': nrm(ks[12], (DEPTH, CONV_DIM), 0.01),
        'dt_bias': dt0 + jnp.log(-jnp.expm1(-dt0)),
        'a_log': jnp.log(jax.random.uniform(ks[13], (DEPTH, SSD_HEADS), f32, 1.0, 16.0)),
        'ssd_d': 1.0 + nrm(ks[14], (DEPTH, SSD_HEADS), 0.01),
        'ssd_norm_g': 1.0 + nrm(ks[15], (DEPTH, D_A), 0.01),
        's5_lambda_re': -0.5 + nrm(ks[16], (DEPTH, S5_GROUPS, S5_STATE), 0.01),
        's5_lambda_im': lam_im0 + nrm(ks[17], (DEPTH, S5_GROUPS, S5_STATE), 0.01),
        's5_log_dt': jax.random.uniform(ks[18], (DEPTH, S5_GROUPS), f32, math.log(DT_MIN), math.log(DT_MAX)),
        's5_b_re': nrm(ks[19], (DEPTH, S5_GROUPS, S5_STATE, S5_CH), (2 * S5_CH) ** -0.5),
        's5_b_im': nrm(ks[20], (DEPTH, S5_GROUPS, S5_STATE, S5_CH), (2 * S5_CH) ** -0.5),
        's5_c_re': nrm(ks[21], (DEPTH, S5_GROUPS, S5_CH, S5_STATE), S5_STATE ** -0.5),
        's5_c_im': nrm(ks[22], (DEPTH, S5_GROUPS, S5_CH, S5_STATE), S5_STATE ** -0.5),
        's5_d': nrm(ks[23], (DEPTH, S5_GROUPS, S5_CH), 1.0),
        'glu_w': nrm(ks[24], (DEPTH, D_B, D_B), D_B ** -0.5),
        'glu_b': nrm(ks[25], (DEPTH, D_B), 0.01),
        'w_out': nrm(ks[26], (DEPTH, D_A + D_B, D_MODEL), (D_A + D_B) ** -0.5),
        'g_ple': 1.0 + nrm(ks[27], (DEPTH, D_MODEL), 0.01),
        'w_ple_gate': nrm(ks[28], (DEPTH, D_MODEL, D_MODEL), D_MODEL ** -0.5),
        'w_ple_proj': nrm(ks[29], (DEPTH, PLE_DIM, D_MODEL), PLE_DIM ** -0.5),
        'g_final': 1.0 + nrm(ks[30], (D_MODEL,), 0.01),
    }


def reference(x_prompt, x_sample, p_prompt, p_sample, state_ssd, state_conv, state_s5_re, state_s5_im,
              w_in, g_in, conv_w, conv_b, dt_bias, a_log, ssd_d, ssd_norm_g,
              s5_lambda_re, s5_lambda_im, s5_log_dt, s5_b_re, s5_b_im, s5_c_re, s5_c_im, s5_d,
              glu_w, glu_b, w_out, g_ple, w_ple_gate, w_ple_proj, g_final):
    f32 = jnp.float32
    bp = x_prompt.shape[0]
    hp = x_prompt
    hs = x_sample
    ssd_p, conv_p, re_p, im_p = [], [], [], []
    ssd_s, conv_s, re_s, im_s = [], [], [], []
    for i in range(DEPTH):
        lw = (w_in[i], g_in[i], conv_w[i], conv_b[i], dt_bias[i], a_log[i], ssd_d[i], ssd_norm_g[i],
              s5_lambda_re[i], s5_lambda_im[i], s5_log_dt[i], s5_b_re[i], s5_b_im[i], s5_c_re[i],
              s5_c_im[i], s5_d[i], glu_w[i], glu_b[i], w_out[i], g_ple[i], w_ple_gate[i], w_ple_proj[i])
        hp, c_new, s_new, r_new, m_new = trunk_layer(
            hp, p_prompt[i],
            jnp.zeros((bp, CONV_K - 1, CONV_DIM), x_prompt.dtype),
            jnp.zeros((bp, SSD_HEADS, SSD_HEAD_DIM, SSD_STATE), f32),
            jnp.zeros((bp, S5_GROUPS, S5_STATE), f32),
            jnp.zeros((bp, S5_GROUPS, S5_STATE), f32),
            *lw)
        ssd_p.append(s_new)
        conv_p.append(c_new)
        re_p.append(r_new)
        im_p.append(m_new)
        hs, c_new, s_new, r_new, m_new = trunk_layer(
            hs, p_sample[i], state_conv[i], state_ssd[i], state_s5_re[i], state_s5_im[i], *lw)
        ssd_s.append(s_new)
        conv_s.append(c_new)
        re_s.append(r_new)
        im_s.append(m_new)
    y_prompt = rmsnorm(hp, g_final)
    y_sample = rmsnorm(hs, g_final)
    return (y_prompt, y_sample,
            jnp.stack(ssd_p), jnp.stack(conv_p), jnp.stack(re_p), jnp.stack(im_p),
            jnp.stack(ssd_s), jnp.stack(conv_s), jnp.stack(re_s), jnp.stack(im_s))
```

```python
import functools
import math

import jax
import jax.numpy as jnp
from jax import lax
from jax.experimental import pallas as pl
from jax.experimental.pallas import tpu as pltpu

F32 = jnp.float32
BF16 = jnp.bfloat16

LANES = 128
SUBLANES = 8
VMEM_LIMIT_BYTES = 56 * 1024 * 1024

D_MODEL = 1024
PLE_DIM = 256
D_A = 1024
D_B = 1024
SSD_HEAD_DIM = 64
SSD_HEADS = D_A // SSD_HEAD_DIM
SSD_STATE = 128
SSD_GROUPS = 2
SSD_HPG = SSD_HEADS // SSD_GROUPS
SSD_GW = SSD_HPG * SSD_HEAD_DIM
CONV_K = 4
CONV_DIM = D_A + 2 * SSD_GROUPS * SSD_STATE
S5_CH = 16
S5_GROUPS = D_B // S5_CH
S5_STATE = 64
S5_LANES = S5_GROUPS * S5_STATE
S5_SLABS = 4
S5_SLAB_IN = D_B // S5_SLABS
S5_SLAB_ST = S5_LANES // S5_SLABS
EPS = 1e-6

SSD_CHUNK = LANES


def _cparams(semantics):
    return pltpu.CompilerParams(dimension_semantics=semantics, vmem_limit_bytes=VMEM_LIMIT_BYTES)


def _resident(shape):
    nd = len(shape)
    return pl.BlockSpec(shape, lambda *_: (0,) * nd, pipeline_mode=pl.Buffered(1))


def _whole(shape):
    nd = len(shape)
    return pl.BlockSpec(shape, lambda *_: (0,) * nd)


def _silu(v):
    return v * jax.nn.sigmoid(v)


def _softplus(v):
    return jnp.maximum(v, 0.0) + jnp.log1p(jnp.exp(-jnp.abs(v)))


def _gelu_tanh(v):
    c = math.sqrt(2.0 / math.pi)
    return 0.5 * v * (1.0 + jnp.tanh(c * (v + 0.044715 * (v * v * v))))


def _rms(v, g):
    ms = jnp.mean(v * v, axis=-1, keepdims=True)
    return v * lax.rsqrt(ms + EPS) * g


def _split3(v):
    hi = v.astype(BF16)
    r1 = v - hi.astype(F32)
    mid = r1.astype(BF16)
    lo = (r1 - mid.astype(F32)).astype(BF16)
    return hi, mid, lo


def _dot(a, b):
    return jnp.dot(a, b, preferred_element_type=F32)


def _exact_dot_left01(sel, v):
    hi, mid, lo = _split3(v)
    return _dot(sel, hi) + _dot(sel, mid) + _dot(sel, lo)


def _exact_dot_right01(v, sel):
    hi, mid, lo = _split3(v)
    return _dot(hi, sel) + _dot(mid, sel) + _dot(lo, sel)


def _inproj_kernel(x_ref, g_ref, wza_ref, wxbc_ref, wdt_ref, wzb_ref, wub_ref,
                   za_ref, xbc_ref, dt_ref, zb_ref, ub_ref):
    n = _rms(x_ref[...], g_ref[...]).astype(BF16)
    za_ref[...] = _dot(n, wza_ref[...])
    xbc_ref[...] = _dot(n, wxbc_ref[...])
    dt_ref[...] = _dot(n, wdt_ref[...])
    zb_ref[...] = _dot(n, wzb_ref[...])
    ub_ref[...] = _dot(n, wub_ref[...])


def _inproj(x2d, nb, nt, g_in, wza, wxbc, wdt, wzb, wub, row_tile):
    steps = nt // row_tile
    nat = lambda b, t: (b * steps + t, 0)
    swp = lambda b, t: (t, b)
    rows = nb * nt
    return pl.pallas_call(
        _inproj_kernel,
        grid=(nb, steps),
        in_specs=[
            pl.BlockSpec((row_tile, D_MODEL), nat),
            _resident((1, D_MODEL)),
            _resident(wza.shape), _resident(wxbc.shape), _resident(wdt.shape),
            _resident(wzb.shape), _resident(wub.shape),
        ],
        out_specs=[
            pl.BlockSpec((row_tile, D_A), nat),
            pl.BlockSpec((row_tile, CONV_DIM), nat),
            pl.BlockSpec((row_tile, LANES), nat),
            pl.BlockSpec((row_tile, D_B), swp),
            pl.BlockSpec((row_tile, D_B), swp),
        ],
        out_shape=[
            jax.ShapeDtypeStruct((rows, D_A), F32),
            jax.ShapeDtypeStruct((rows, CONV_DIM), F32),
            jax.ShapeDtypeStruct((rows, LANES), F32),
            jax.ShapeDtypeStruct((nt, nb * D_B), F32),
            jax.ShapeDtypeStruct((nt, nb * D_B), F32),
        ],
        compiler_params=_cparams(("arbitrary", "arbitrary")),
        name="inproj",
    )(x2d, g_in, wza, wxbc, wdt, wzb, wub)


def _ssd_gate_norm(y, za, ng):
    y = y * _silu(za)
    parts = []
    for g in range(SSD_GROUPS):
        sl = slice(g * SSD_GW, (g + 1) * SSD_GW)
        parts.append(_rms(y[:, sl], ng[:, sl]))
    return jnp.concatenate(parts, axis=-1)


def _ssd_prompt_kernel(xbc_ref, dt_ref, za_ref, cw_ref, cb_ref, dtb_ref, alog_ref, dexp_ref,
                       ng_ref, e16_ref, ebc_ref,
                       ya_ref, hout_ref, cout_ref, xext, hst):
    L = SSD_CHUNK
    c = pl.program_id(1)

    @pl.when(c == 0)
    def _():
        xext[0:SUBLANES, :] = jnp.zeros((SUBLANES, CONV_DIM), F32)
        hst[...] = jnp.zeros(hst.shape, F32)

    xext[SUBLANES:SUBLANES + L, :] = xbc_ref[...]
    acc = cb_ref[...] + cw_ref[CONV_K - 1:CONV_K, :] * xext[SUBLANES:SUBLANES + L, :]
    for k in range(1, CONV_K):
        acc = acc + cw_ref[CONV_K - 1 - k:CONV_K - k, :] * xext[SUBLANES - k:SUBLANES - k + L, :]
    xc = _silu(acc)
    xext[0:SUBLANES, :] = xext[L:L + SUBLANES, :]

    xs = xc[:, :D_A]
    bm = xc[:, D_A:D_A + SSD_GROUPS * SSD_STATE]
    cm = xc[:, D_A + SSD_GROUPS * SSD_STATE:]

    dt = _softplus(dt_ref[...] + dtb_ref[...])
    a_neg = -jnp.exp(alog_ref[...])
    da = dt * a_neg
    row = lax.broadcasted_iota(jnp.int32, (L, L), 0)
    col = lax.broadcasted_iota(jnp.int32, (L, L), 1)
    causal = row >= col
    tri = jnp.where(causal, 1.0, 0.0).astype(BF16)
    acs = _exact_dot_left01(tri, da)
    acs_last = acs[L - 1:L, :]
    ea = jnp.exp(acs)
    wd = jnp.exp(acs_last - acs) * dt

    e16 = e16_ref[...]
    dt_x = _exact_dot_right01(dt, e16)
    wd_x = _exact_dot_right01(wd, e16)
    ea_x = _exact_dot_right01(ea, e16)
    acs_b = _exact_dot_right01(acs, ebc_ref[...])
    acs_t = acs.T

    xdt = xs * dt_x
    xw = (xs * wd_x).astype(BF16)

    lane = lax.broadcasted_iota(jnp.int32, (L, LANES), 1)
    left = lane < SSD_HEAD_DIM

    ydiag = []
    yoff = []
    for g in range(SSD_GROUPS):
        cg = cm[:, g * SSD_STATE:(g + 1) * SSD_STATE].astype(BF16)
        bg32 = bm[:, g * SSD_STATE:(g + 1) * SSD_STATE]
        bg = bg32.astype(BF16)
        cbm = lax.dot_general(cg, bg, (((1,), (1,)), ((), ())), preferred_element_type=F32)
        for j in range(SSD_HPG // 2):
            q = g * (SSD_HPG // 2) + j
            pair = xdt[:, q * LANES:(q + 1) * LANES]
            rhs = [jnp.where(left, pair, 0.0).astype(BF16), jnp.where(left, 0.0, pair).astype(BF16)]
            ms = []
            for i in range(2):
                h = 2 * q + i
                seg = acs_b[:, h * LANES:(h + 1) * LANES] - acs_t[h:h + 1, :]
                dec = jnp.exp(jnp.where(causal, seg, -jnp.inf))
                ms.append((cbm * dec).astype(BF16))
            ydiag.append(_dot(jnp.concatenate(ms, axis=1), jnp.concatenate(rhs, axis=0)))
        hg = hst[g]
        yoff.append(_dot(cg, hg.astype(BF16)))
        cd = ea_x[L - 1:L, g * SSD_GW:(g + 1) * SSD_GW]
        upd = _dot(bg32.T.astype(BF16), xw[:, g * SSD_GW:(g + 1) * SSD_GW])
        hst[g] = hg * cd + upd

    y = jnp.concatenate(ydiag, axis=1) + jnp.concatenate(yoff, axis=1) * ea_x + dexp_ref[...] * xs
    ya_ref[...] = _ssd_gate_norm(y, za_ref[...], ng_ref[...])

    @pl.when(c == pl.num_programs(1) - 1)
    def _():
        for g in range(SSD_GROUPS):
            hout_ref[0, g * SSD_GW:(g + 1) * SSD_GW, :] = hst[g].T
        cout_ref[0] = xext[SUBLANES - (CONV_K - 1):SUBLANES, :]


def _ssd_prompt(xbc, dt, za, nb, nt, cw, cb, dtb, alog, dexp, ng, e16, ebc):
    L = SSD_CHUNK
    nc = nt // L
    nat = lambda b, c: (b * nc + c, 0)
    return pl.pallas_call(
        _ssd_prompt_kernel,
        grid=(nb, nc),
        in_specs=[
            pl.BlockSpec((L, CONV_DIM), nat),
            pl.BlockSpec((L, LANES), nat),
            pl.BlockSpec((L, D_A), nat),
            _resident(cw.shape), _resident(cb.shape), _resident(dtb.shape), _resident(alog.shape),
            _resident(dexp.shape), _resident(ng.shape), _resident(e16.shape), _resident(ebc.shape),
        ],
        out_specs=[
            pl.BlockSpec((L, D_A), nat),
            pl.BlockSpec((1, D_A, SSD_STATE), lambda b, c: (b, 0, 0)),
            pl.BlockSpec((1, CONV_K - 1, CONV_DIM), lambda b, c: (b, 0, 0)),
        ],
        out_shape=[
            jax.ShapeDtypeStruct((nb * nt, D_A), F32),
            jax.ShapeDtypeStruct((nb, D_A, SSD_STATE), F32),
            jax.ShapeDtypeStruct((nb, CONV_K - 1, CONV_DIM), F32),
        ],
        scratch_shapes=[
            pltpu.VMEM((L + SUBLANES, CONV_DIM), F32),
            pltpu.VMEM((SSD_GROUPS, SSD_STATE, SSD_GW), F32),
        ],
        compiler_params=_cparams(("arbitrary", "arbitrary")),
        name="ssd_prompt",
    )(xbc, dt, za, cw, cb, dtb, alog, dexp, ng, e16, ebc)


def _ssd_step_pre_kernel(nb, nt, xbc_ref, cbuf_ref, dt_ref, cw_ref, cb_ref, dtb_ref, alog_ref,
                         dexp_ref, e16_ref,
                         ypart_ref, cm_ref, bm_ref, xw_ref, eax_ref, cd_ref):
    full = [cbuf_ref[k * nb:(k + 1) * nb, :] for k in range(CONV_K - 1)]
    full += [xbc_ref[t * nb:(t + 1) * nb, :] for t in range(nt)]
    e16 = e16_ref[...]
    a_neg = -jnp.exp(alog_ref[...])
    lane = lax.broadcasted_iota(jnp.int32, (nb, LANES), 1)
    first_group = lane < SSD_HPG

    xs, bm, cm, dts, acs = [], [], [], [], []
    run = None
    for t in range(nt):
        acc = cb_ref[...] + cw_ref[0:1, :] * full[t]
        for k in range(1, CONV_K):
            acc = acc + cw_ref[k:k + 1, :] * full[t + k]
        xc = _silu(acc)
        xs.append(xc[:, :D_A])
        bm.append(xc[:, D_A:D_A + SSD_GROUPS * SSD_STATE])
        cm.append(xc[:, D_A + SSD_GROUPS * SSD_STATE:])
        d = _softplus(dt_ref[t * nb:(t + 1) * nb, :] + dtb_ref[...])
        dts.append(d)
        run = d * a_neg if run is None else run + d * a_neg
        acs.append(run)

    for l in range(nt):
        y = dexp_ref[...] * xs[l]
        for s in range(l + 1):
            cbv = []
            for g in range(SSD_GROUPS):
                sl = slice(g * SSD_STATE, (g + 1) * SSD_STATE)
                cbv.append(jnp.sum(cm[l][:, sl] * bm[s][:, sl], axis=-1, keepdims=True))
            cbfull = jnp.where(first_group, cbv[0], cbv[1])
            coef = cbfull * jnp.exp(acs[l] - acs[s]) * dts[s]
            y = y + _exact_dot_right01(coef, e16) * xs[s]
        rows = slice(l * nb, (l + 1) * nb)
        ypart_ref[rows, :] = y
        cm_ref[rows, :] = cm[l]
        bm_ref[rows, :] = bm[l]
        wd = jnp.exp(acs[nt - 1] - acs[l]) * dts[l]
        xw_ref[rows, :] = xs[l] * _exact_dot_right01(wd, e16)
        eax_ref[rows, :] = _exact_dot_right01(jnp.exp(acs[l]), e16)
    cd_ref[...] = jnp.exp(acs[nt - 1])


def _ssd_step_pre(xbc, cbuf, dt, nb, nt, cw, cb, dtb, alog, dexp, e16):
    rows = nb * nt
    args = (xbc, cbuf, dt, cw, cb, dtb, alog, dexp, e16)
    return pl.pallas_call(
        functools.partial(_ssd_step_pre_kernel, nb, nt),
        grid=(1,),
        in_specs=[_resident(a.shape) for a in args],
        out_specs=[
            _whole((rows, D_A)), _whole((rows, SSD_GROUPS * SSD_STATE)),
            _whole((rows, SSD_GROUPS * SSD_STATE)), _whole((rows, D_A)),
            _whole((rows, D_A)), _whole((nb, LANES)),
        ],
        out_shape=[
            jax.ShapeDtypeStruct((rows, D_A), F32),
            jax.ShapeDtypeStruct((rows, SSD_GROUPS * SSD_STATE), F32),
            jax.ShapeDtypeStruct((rows, SSD_GROUPS * SSD_STATE), F32),
            jax.ShapeDtypeStruct((rows, D_A), F32),
            jax.ShapeDtypeStruct((rows, D_A), F32),
            jax.ShapeDtypeStruct((nb, LANES), F32),
        ],
        compiler_params=_cparams(("arbitrary",)),
        name="ssd_step_pre",
    )(*args)


def _ssd_step_state_kernel(nt, cd_ref, cm_ref, bm_ref, xw_ref, eax_ref, ypart_ref, za_ref, ng_ref,
                           h0_ref, ya_ref, hnew_ref, pad_a, pad_b):
    b = pl.program_id(0)
    pad_a[...] = jnp.zeros(pad_a.shape, F32)
    pad_b[...] = jnp.zeros(pad_b.shape, F32)
    pad_a[0:nt, :] = xw_ref[...]
    pad_b[0:nt, :] = bm_ref[...]
    yoff = []
    for g in range(SSD_GROUPS):
        cg = cm_ref[:, g * SSD_STATE:(g + 1) * SSD_STATE].astype(BF16)
        hg = h0_ref[0, g * SSD_GW:(g + 1) * SSD_GW, :]
        yoff.append(lax.dot_general(cg, hg.astype(BF16), (((1,), (1,)), ((), ())),
                                    preferred_element_type=F32))
        xw_t = pad_a[:, g * SSD_GW:(g + 1) * SSD_GW].T.astype(BF16)
        upd = _dot(xw_t, pad_b[:, g * SSD_STATE:(g + 1) * SSD_STATE].astype(BF16))
        for e in range(SSD_HPG):
            r0 = g * SSD_GW + e * SSD_HEAD_DIM
            decay = cd_ref[b, g * SSD_HPG + e]
            hnew_ref[0, r0:r0 + SSD_HEAD_DIM, :] = (
                h0_ref[0, r0:r0 + SSD_HEAD_DIM, :] * decay
                + upd[e * SSD_HEAD_DIM:(e + 1) * SSD_HEAD_DIM, :])
    y = ypart_ref[...] + jnp.concatenate(yoff, axis=1) * eax_ref[...]
    ya_ref[...] = _ssd_gate_norm(y, za_ref[...], ng_ref[...])


def _ssd_step_state(cd, cm, bm, xw, eax, ypart, za, ng, h0, nb, nt):
    blk = lambda w: pl.BlockSpec((nt, w), lambda b: (0, b))
    gn = SSD_GROUPS * SSD_STATE
    return pl.pallas_call(
        functools.partial(_ssd_step_state_kernel, nt),
        grid=(nb,),
        in_specs=[
            pl.BlockSpec(memory_space=pltpu.SMEM),
            blk(gn), blk(gn), blk(D_A), blk(D_A), blk(D_A), blk(D_A),
            _resident(ng.shape),
            pl.BlockSpec((1, D_A, SSD_STATE), lambda b: (b, 0, 0)),
        ],
        out_specs=[
            blk(D_A),
            pl.BlockSpec((1, D_A, SSD_STATE), lambda b: (b, 0, 0)),
        ],
        out_shape=[
            jax.ShapeDtypeStruct((nt, nb * D_A), F32),
            jax.ShapeDtypeStruct((nb, D_A, SSD_STATE), F32),
        ],
        scratch_shapes=[
            pltpu.VMEM((LANES, D_A), F32),
            pltpu.VMEM((LANES, gn), F32),
        ],
        compiler_params=_cparams(("arbitrary",)),
        name="ssd_step_state",
    )(cd, cm, bm, xw, eax, ypart, za, ng, h0)


def _s5_kernel(nb, ts, u_ref, zb_ref, h0re_ref, h0im_ref, are_ref, aim_ref, bs_ref, cs_ref, d_ref,
               gluw_ref, glub_ref,
               yb_ref, hre_ref, him_ref, bu, sre, sim, yacc):
    i = pl.program_id(0)

    @pl.when(i == 0)
    def _():
        sre[...] = h0re_ref[...]
        sim[...] = h0im_ref[...]

    u = u_ref[...]
    ub = u.astype(BF16)
    st = S5_SLAB_ST
    for s in range(S5_SLABS):
        cin = slice(s * S5_SLAB_IN, (s + 1) * S5_SLAB_IN)
        cst = slice(s * st, (s + 1) * st)
        bu[...] = _dot(ub[:, cin], bs_ref[s])
        ar = jnp.broadcast_to(are_ref[:, cst], (SUBLANES, st))
        ai = jnp.broadcast_to(aim_ref[:, cst], (SUBLANES, st))
        for bg in range(nb // SUBLANES):
            rb = slice(bg * SUBLANES, (bg + 1) * SUBLANES)

            def step(t, carry, ar=ar, ai=ai, bg=bg):
                xr, xi = carry
                r0 = pl.multiple_of(t * nb + bg * SUBLANES, SUBLANES)
                nxr = ar * xr - ai * xi + bu[pl.ds(r0, SUBLANES), 0:st]
                nxi = ar * xi + ai * xr + bu[pl.ds(r0, SUBLANES), st:2 * st]
                bu[pl.ds(r0, SUBLANES), 0:st] = nxr
                bu[pl.ds(r0, SUBLANES), st:2 * st] = nxi
                return nxr, nxi

            xr, xi = lax.fori_loop(0, ts, step, (sre[rb, cst], sim[rb, cst]), unroll=min(ts, 4))
            sre[rb, cst] = xr
            sim[rb, cst] = xi
        yacc[:, cin] = _dot(bu[...].astype(BF16), cs_ref[s]) + d_ref[:, cin] * u[:, cin]

    g = _gelu_tanh(yacc[...])
    gate = jax.nn.sigmoid(_dot(g.astype(BF16), gluw_ref[...]) + glub_ref[...])
    yb_ref[...] = g * gate * _silu(zb_ref[...])

    @pl.when(i == pl.num_programs(0) - 1)
    def _():
        hre_ref[...] = sre[...]
        him_ref[...] = sim[...]


def _s5(u, zb, h0re, h0im, a_re, a_im, bs, cs, dvec, gluw, glub, nb, nt, ts):
    rows = nb * ts
    steps = nt // ts
    blk = pl.BlockSpec((rows, D_B), lambda i: (i, 0))
    return pl.pallas_call(
        functools.partial(_s5_kernel, nb, ts),
        grid=(steps,),
        in_specs=[
            blk, blk,
            _resident(h0re.shape), _resident(h0im.shape),
            _resident(a_re.shape), _resident(a_im.shape),
            _resident(bs.shape), _resident(cs.shape), _resident(dvec.shape),
            _resident(gluw.shape), _resident(glub.shape),
        ],
        out_specs=[blk, _whole((nb, S5_LANES)), _whole((nb, S5_LANES))],
        out_shape=[
            jax.ShapeDtypeStruct((nb * nt, D_B), F32),
            jax.ShapeDtypeStruct((nb, S5_LANES), F32),
            jax.ShapeDtypeStruct((nb, S5_LANES), F32),
        ],
        scratch_shapes=[
            pltpu.VMEM((rows, 2 * S5_SLAB_ST), F32),
            pltpu.VMEM((nb, S5_LANES), F32),
            pltpu.VMEM((nb, S5_LANES), F32),
            pltpu.VMEM((rows, D_B), F32),
        ],
        compiler_params=_cparams(("arbitrary",)),
        name="s5",
    )(u, zb, h0re, h0im, a_re, a_im, bs, cs, dvec, gluw, glub)


def _outproj_kernel(x_ref, ya_ref, yb_ref, p_ref, woa_ref, wob_ref, gple_ref, wg_ref, wp_ref,
                    gfin_ref, y_ref):
    h = x_ref[...] + _dot(ya_ref[...].astype(BF16), woa_ref[...]) \
        + _dot(yb_ref[...].astype(BF16), wob_ref[...])
    gate = jax.nn.sigmoid(_dot(_rms(h, gple_ref[...]).astype(BF16), wg_ref[...]))
    h = h + _dot(p_ref[...].astype(BF16), wp_ref[...]) * gate
    y_ref[...] = _rms(h, gfin_ref[...])


def _outproj(x2d, ya, yb2d, p2d, nb, nt, woa, wob, gple, wg, wp, gfin, row_tile):
    steps = nt // row_tile
    nat = lambda b, t: (b * steps + t, 0)
    return pl.pallas_call(
        _outproj_kernel,
        grid=(nb, steps),
        in_specs=[
            pl.BlockSpec((row_tile, D_MODEL), nat),
            pl.BlockSpec((row_tile, D_A), nat),
            pl.BlockSpec((row_tile, D_B), lambda b, t: (t, b)),
            pl.BlockSpec((row_tile, PLE_DIM), nat),
            _resident(woa.shape), _resident(wob.shape), _resident(gple.shape),
            _resident(wg.shape), _resident(wp.shape), _resident(gfin.shape),
        ],
        out_specs=pl.BlockSpec((row_tile, D_MODEL), nat),
        out_shape=jax.ShapeDtypeStruct((nb * nt, D_MODEL), F32),
        compiler_params=_cparams(("arbitrary", "arbitrary")),
        name="outproj",
    )(x2d, ya, yb2d, p2d, woa, wob, gple, wg, wp, gfin)


def _pad_lanes(v, width=LANES):
    return jnp.pad(v, ((0, 0), (0, width - v.shape[-1])))


def _s5_params(lam_re, lam_im, log_dt, b_re, b_im, c_re, c_im):
    lr = lam_re.astype(F32)
    li = lam_im.astype(F32)
    delta = jnp.exp(log_dt.astype(F32))[:, None]
    mag = jnp.exp(lr * delta)
    ab_re = mag * jnp.cos(li * delta)
    ab_im = mag * jnp.sin(li * delta)
    q_re = ab_re - 1.0
    den = lr * lr + li * li
    f_re = (q_re * lr + ab_im * li) / den
    f_im = (ab_im * lr - q_re * li) / den
    bb_re = f_re[..., None] * b_re.astype(F32) - f_im[..., None] * b_im.astype(F32)
    bb_im = f_re[..., None] * b_im.astype(F32) + f_im[..., None] * b_re.astype(F32)
    gs = S5_GROUPS // S5_SLABS
    eye = jnp.eye(gs, dtype=F32)

    def in_slab(w):
        w = w.reshape(S5_SLABS, gs, S5_STATE, S5_CH)
        return jnp.einsum("sgph,gk->sghkp", w, eye).reshape(S5_SLABS, gs * S5_CH, gs * S5_STATE)

    def out_slab(w):
        w = w.reshape(S5_SLABS, gs, S5_CH, S5_STATE)
        return jnp.einsum("sghp,gk->sgpkh", w, eye).reshape(S5_SLABS, gs * S5_STATE, gs * S5_CH)

    bs = jnp.concatenate([in_slab(bb_re), in_slab(bb_im)], axis=2).astype(BF16)
    cs = jnp.concatenate([out_slab(c_re.astype(F32)), out_slab(-c_im.astype(F32))], axis=1).astype(BF16)
    return ab_re.reshape(1, S5_LANES), ab_im.reshape(1, S5_LANES), bs, cs


def _layer(i, x_prompt, x_sample_tb, p_prompt, p_sample_tb, state_ssd, state_conv, state_s5_re,
           state_s5_im, w_in, g_in, conv_w, conv_b, dt_bias, a_log, ssd_d, ssd_norm_g,
           s5_lambda_re, s5_lambda_im, s5_log_dt, s5_b_re, s5_b_im, s5_c_re, s5_c_im, s5_d,
           glu_w, glu_b, w_out, g_ple, w_ple_gate, w_ple_proj, g_final):
    bp, tp, _ = x_prompt.shape
    ts_, bs_, _ = x_sample_tb.shape

    i1 = D_A
    i2 = i1 + CONV_DIM
    i3 = i2 + SSD_HEADS
    i4 = i3 + D_B
    w = w_in[i].astype(BF16)
    wza, wxbc, wzb, wub = w[:, :i1], w[:, i1:i2], w[:, i3:i4], w[:, i4:]
    wdt = _pad_lanes(w[:, i2:i3])
    gin = g_in[i].reshape(1, D_MODEL)
    cw = conv_w[i]
    cb = conv_b[i].reshape(1, CONV_DIM)
    dtb = _pad_lanes(dt_bias[i].reshape(1, SSD_HEADS))
    alog = _pad_lanes(a_log[i].reshape(1, SSD_HEADS))
    dexp = jnp.repeat(ssd_d[i], SSD_HEAD_DIM).reshape(1, D_A)
    ng = ssd_norm_g[i].reshape(1, D_A)
    head_of_col = jnp.arange(D_A) // SSD_HEAD_DIM
    e16 = (jnp.arange(LANES)[:, None] == head_of_col[None, :]).astype(BF16)
    blk_of_col = jnp.arange(SSD_HEADS * LANES) // LANES
    ebc = (jnp.arange(LANES)[:, None] == blk_of_col[None, :]).astype(BF16)
    a_re, a_im, bs, cs = _s5_params(s5_lambda_re[i], s5_lambda_im[i], s5_log_dt[i],
                                    s5_b_re[i], s5_b_im[i], s5_c_re[i], s5_c_im[i])
    dvec = s5_d[i].reshape(1, D_B)
    gluw = glu_w[i].astype(BF16)
    glub = glu_b[i].reshape(1, D_B)
    wo = w_out[i].astype(BF16)
    woa, wob = wo[:D_A], wo[D_A:]
    gple = g_ple[i].reshape(1, D_MODEL)
    wg = w_ple_gate[i].astype(BF16)
    wp = w_ple_proj[i].astype(BF16)
    gfin = g_final.reshape(1, D_MODEL)

    xp2d = x_prompt.reshape(bp * tp, D_MODEL)
    row_tile = min(tp, 512)
    za, xbc, dt, zb, ub = _inproj(xp2d, bp, tp, gin, wza, wxbc, wdt, wzb, wub, row_tile=row_tile)
    ya, ssd_p, conv_p = _ssd_prompt(xbc, dt, za, bp, tp, cw, cb, dtb, alog, dexp, ng, e16, ebc)
    zeros_s5 = jnp.zeros((bp, S5_LANES), F32)
    yb, re_p, im_p = _s5(ub.reshape(tp * bp, D_B), zb.reshape(tp * bp, D_B), zeros_s5, zeros_s5,
                         a_re, a_im, bs, cs, dvec, gluw, glub, bp, tp, ts=min(tp, 64))
    hp = _outproj(xp2d, ya, yb.reshape(tp, bp * D_B), p_prompt[i].reshape(bp * tp, PLE_DIM),
                  bp, tp, woa, wob, gple, wg, wp, gfin, row_tile=row_tile)
    out_p = (hp.reshape(bp, tp, D_MODEL),
             ssd_p.reshape(bp, SSD_HEADS, SSD_HEAD_DIM, SSD_STATE), conv_p,
             re_p.reshape(bp, S5_GROUPS, S5_STATE), im_p.reshape(bp, S5_GROUPS, S5_STATE))

    rows = ts_ * bs_
    xs2d = x_sample_tb.reshape(rows, D_MODEL)
    za, xbc, dt, zb, ub = _inproj(xs2d, 1, rows, gin, wza, wxbc, wdt, wzb, wub, row_tile=rows)
    cbuf = jnp.transpose(state_conv[i], (1, 0, 2)).reshape((CONV_K - 1) * bs_, CONV_DIM)
    ypart, cm, bm, xw, eax, cd = _ssd_step_pre(xbc, cbuf, dt, bs_, ts_, cw, cb, dtb, alog, dexp, e16)
    v2 = lambda a: a.reshape(ts_, bs_ * a.shape[-1])
    ya, ssd_s = _ssd_step_state(cd[:, :SSD_HEADS], v2(cm), v2(bm), v2(xw), v2(eax), v2(ypart), v2(za), ng,
                                state_ssd[i].reshape(bs_, D_A, SSD_STATE), bs_, ts_)
    yb, re_s, im_s = _s5(ub, zb, state_s5_re[i].reshape(bs_, S5_LANES),
                         state_s5_im[i].reshape(bs_, S5_LANES),
                         a_re, a_im, bs, cs, dvec, gluw, glub, bs_, ts_, ts=ts_)
    hs = _outproj(xs2d, ya.reshape(rows, D_A), yb, p_sample_tb[i].reshape(rows, PLE_DIM),
                  1, rows, woa, wob, gple, wg, wp, gfin, row_tile=rows)
    hist = jnp.concatenate([cbuf.reshape(CONV_K - 1, bs_, CONV_DIM), xbc.reshape(ts_, bs_, CONV_DIM)], axis=0)
    conv_s = jnp.transpose(hist[-(CONV_K - 1):], (1, 0, 2))
    out_s = (hs.reshape(ts_, bs_, D_MODEL),
             ssd_s.reshape(bs_, SSD_HEADS, SSD_HEAD_DIM, SSD_STATE), conv_s,
             re_s.reshape(bs_, S5_GROUPS, S5_STATE), im_s.reshape(bs_, S5_GROUPS, S5_STATE))
    return out_p, out_s


def kernel(x_prompt, x_sample, p_prompt, p_sample, state_ssd, state_conv, state_s5_re, state_s5_im, w_in, g_in, conv_w, conv_b, dt_bias, a_log, ssd_d, ssd_norm_g, s5_lambda_re, s5_lambda_im, s5_log_dt, s5_b_re, s5_b_im, s5_c_re, s5_c_im, s5_d, glu_w, glu_b, w_out, g_ple, w_ple_gate, w_ple_proj, g_final):
    depth = w_in.shape[0]
    assert depth == 1, "the final RMSNorm is fused into the single layer's epilogue"
    xs_tb = jnp.transpose(x_sample, (1, 0, 2))
    ps_tb = jnp.transpose(p_sample, (0, 2, 1, 3))
    out_p, out_s = _layer(0, x_prompt, xs_tb, p_prompt, ps_tb, state_ssd, state_conv, state_s5_re,
                          state_s5_im, w_in, g_in, conv_w, conv_b, dt_bias, a_log, ssd_d, ssd_norm_g,
                          s5_lambda_re, s5_lambda_im, s5_log_dt, s5_b_re, s5_b_im, s5_c_re, s5_c_im,
                          s5_d, glu_w, glu_b, w_out, g_ple, w_ple_gate, w_ple_proj, g_final)
    y_sample = jnp.transpose(out_s[0], (1, 0, 2))
    stack = lambda a: a[None]
    return (out_p[0], y_sample,
            stack(out_p[1]), stack(out_p[2]), stack(out_p[3]), stack(out_p[4]),
            stack(out_s[1]), stack(out_s[2]), stack(out_s[3]), stack(out_s[4]))
```

```python
import functools
import math

import jax
import jax.numpy as jnp
from jax import lax
from jax.experimental import pallas as pl
from jax.experimental.pallas import tpu as pltpu

F32 = jnp.float32
BF16 = jnp.bfloat16

LANES = 128
SUBLANES = 8
VMEM_LIMIT_BYTES = 56 * 1024 * 1024

D_MODEL = 1024
PLE_DIM = 256
D_A = 1024
D_B = 1024
SSD_HEAD_DIM = 64
SSD_HEADS = D_A // SSD_HEAD_DIM
SSD_STATE = 128
SSD_GROUPS = 2
SSD_HPG = SSD_HEADS // SSD_GROUPS
SSD_GW = SSD_HPG * SSD_HEAD_DIM
SSD_BC = SSD_GROUPS * SSD_STATE
CONV_K = 4
CONV_DIM = D_A + 2 * SSD_BC
S5_CH = 16
S5_GROUPS = D_B // S5_CH
S5_STATE = 64
S5_LANES = S5_GROUPS * S5_STATE
S5_SLABS = 4
S5_SLAB_IN = D_B // S5_SLABS
S5_SLAB_ST = S5_LANES // S5_SLABS
EPS = 1e-6

HEAD_REPLICAS = 3

SSD_CHUNK = LANES
SSD_SEQS_PER_STEP = 2
TIME_TILE = 64
STATE_SEQS_PER_STEP = 8


def _cparams(semantics):
    return pltpu.CompilerParams(dimension_semantics=semantics, vmem_limit_bytes=VMEM_LIMIT_BYTES)


def _resident(shape):
    nd = len(shape)
    return pl.BlockSpec(shape, lambda *_: (0,) * nd, pipeline_mode=pl.Buffered(1))


def _whole(shape):
    nd = len(shape)
    return pl.BlockSpec(shape, lambda *_: (0,) * nd)


def _silu(v):
    return v * jax.nn.sigmoid(v)


def _softplus(v):
    return jnp.maximum(v, 0.0) + jnp.log1p(jnp.exp(-jnp.abs(v)))


def _gelu_tanh(v):
    c = math.sqrt(2.0 / math.pi)
    return 0.5 * v * (1.0 + jnp.tanh(c * (v + 0.044715 * (v * v * v))))


def _rms(v, g):
    ms = jnp.mean(v * v, axis=-1, keepdims=True)
    return v * lax.rsqrt(ms + EPS) * g


def _dot(a, b):
    return jnp.dot(a, b, preferred_element_type=F32)


def _split3(v):
    hi = v.astype(BF16).astype(F32)
    r1 = v - hi
    mid = r1.astype(BF16).astype(F32)
    lo = r1 - mid
    return hi, mid, lo


def _cumsum_rows(tri, v):
    hi, mid, lo = _split3(v)
    return _dot(tri, hi.astype(BF16)) + _dot(tri, mid.astype(BF16)) + _dot(tri, lo.astype(BF16))


def _pack3(v):
    hi, mid, lo = _split3(v)
    lane = lax.broadcasted_iota(jnp.int32, v.shape, 1)
    return jnp.where(lane < SSD_HEADS, hi, jnp.where(lane < 2 * SSD_HEADS, mid, lo)).astype(BF16)


def _spread(v, sel):
    return _dot(_pack3(v), sel)


def _inproj_kernel(permute, *refs):
    if permute:
        x_ref, g_ref, perm_ref, wza_ref, wxbc_ref, wdt_ref, wzb_ref, wub_ref = refs[:8]
    else:
        x_ref, g_ref, wza_ref, wxbc_ref, wdt_ref, wzb_ref, wub_ref = refs[:7]
    za_ref, xbc_ref, dt_ref, zb_ref, ub_ref = refs[-5:]
    nb, tt, _ = x_ref.shape
    n = _rms(x_ref[...].reshape(nb * tt, D_MODEL), g_ref[...]).astype(BF16)
    za_ref[...] = _dot(n, wza_ref[...]).reshape(nb, tt, D_A)
    xbc_ref[...] = _dot(n, wxbc_ref[...]).reshape(nb, tt, CONV_DIM)
    dt_ref[...] = _dot(n, wdt_ref[...]).reshape(nb, tt, LANES)
    n_tb = _dot(perm_ref[...], n).astype(BF16) if permute else n
    zb_ref[...] = _dot(n_tb, wzb_ref[...])
    ub_ref[...] = _dot(n_tb, wub_ref[...])


def _inproj(x3d, perm, g_in, wza, wxbc, wdt, wzb, wub, tt):
    nb, nt, _ = x3d.shape
    rows = nb * tt
    blk3 = lambda w: pl.BlockSpec((nb, tt, w), lambda i: (0, i, 0))
    blk2 = pl.BlockSpec((rows, D_B), lambda i: (i, 0))
    weights = (wza, wxbc, wdt, wzb, wub)
    args = (x3d, g_in) + ((perm,) if perm is not None else ()) + weights
    in_specs = [blk3(D_MODEL), _resident(g_in.shape)]
    in_specs += [_resident(perm.shape)] if perm is not None else []
    in_specs += [_resident(w.shape) for w in weights]
    return pl.pallas_call(
        functools.partial(_inproj_kernel, perm is not None),
        grid=(nt // tt,),
        in_specs=in_specs,
        out_specs=[blk3(D_A), blk3(CONV_DIM), blk3(LANES), blk2, blk2],
        out_shape=[
            jax.ShapeDtypeStruct((nb, nt, D_A), F32),
            jax.ShapeDtypeStruct((nb, nt, CONV_DIM), F32),
            jax.ShapeDtypeStruct((nb, nt, LANES), F32),
            jax.ShapeDtypeStruct((nb * nt, D_B), F32),
            jax.ShapeDtypeStruct((nb * nt, D_B), F32),
        ],
        compiler_params=_cparams(("arbitrary",)),
        name="inproj",
    )(*args)


def _ssd_gate_norm(y, za, ng):
    y = y * _silu(za)
    parts = []
    for g in range(SSD_GROUPS):
        sl = slice(g * SSD_GW, (g + 1) * SSD_GW)
        parts.append(_rms(y[:, sl], ng[:, sl]))
    return jnp.concatenate(parts, axis=-1)


def _ssd_chunk(xext, hst, xbc, dt_raw, za, cw, cb, dtb, a_neg, dexp, ng, e48, ebc, tri, causal, left):
    L = SSD_CHUNK
    xext[SUBLANES:SUBLANES + L, :] = xbc
    ext = xext[...]
    acc = cw[0:1, :] * ext
    for k in range(1, CONV_K):
        acc = cw[k:k + 1, :] * ext + pltpu.roll(acc, 1, axis=0)
    xc = _silu(acc[SUBLANES:, :] + cb)
    xext[0:SUBLANES, :] = ext[L:L + SUBLANES, :]

    xs = xc[:, :D_A]
    bm = xc[:, D_A:D_A + SSD_BC]
    cm = xc[:, D_A + SSD_BC:]

    dt = _softplus(dt_raw + dtb)
    acs = _cumsum_rows(tri, dt * a_neg)
    ea = jnp.exp(acs)
    wd = jnp.exp(acs[L - 1:L, :] - acs) * dt
    dt_x = _spread(dt, e48)
    wd_x = _spread(wd, e48)
    ea_x = _spread(ea, e48)
    acs_b = _spread(acs, ebc)
    acs_t = acs.T

    xdt = xs * dt_x
    xw = (xs * wd_x).astype(BF16)

    ydiag = []
    yoff = []
    for g in range(SSD_GROUPS):
        cg = cm[:, g * SSD_STATE:(g + 1) * SSD_STATE].astype(BF16)
        bg32 = bm[:, g * SSD_STATE:(g + 1) * SSD_STATE]
        bg = bg32.astype(BF16)
        cbm = lax.dot_general(cg, bg, (((1,), (1,)), ((), ())), preferred_element_type=F32)
        for j in range(SSD_HPG // 2):
            q = g * (SSD_HPG // 2) + j
            pair = xdt[:, q * LANES:(q + 1) * LANES]
            rhs = [jnp.where(left, pair, 0.0).astype(BF16), jnp.where(left, 0.0, pair).astype(BF16)]
            ms = []
            for i in range(2):
                h = 2 * q + i
                seg = acs_b[:, h * LANES:(h + 1) * LANES] - acs_t[h:h + 1, :]
                dec = jnp.exp(jnp.where(causal, seg, -jnp.inf))
                ms.append((cbm * dec).astype(BF16))
            ydiag.append(_dot(jnp.concatenate(ms, axis=1), jnp.concatenate(rhs, axis=0)))
        hg = hst[g]
        yoff.append(_dot(cg, hg.astype(BF16)))
        cd = ea_x[L - 1:L, g * SSD_GW:(g + 1) * SSD_GW]
        hst[g] = hg * cd + _dot(bg32.T.astype(BF16), xw[:, g * SSD_GW:(g + 1) * SSD_GW])

    y = jnp.concatenate(ydiag, axis=1) + jnp.concatenate(yoff, axis=1) * ea_x + dexp * xs
    return _ssd_gate_norm(y, za, ng)


def _ssd_prompt_kernel(xbc_ref, dt_ref, za_ref, cw_ref, cb_ref, dtb_ref, alog_ref, dexp_ref,
                       ng_ref, e48_ref, ebc_ref,
                       ya_ref, hout_ref, cout_ref, xext, hst):
    L = SSD_CHUNK
    ns = xbc_ref.shape[0]
    c = pl.program_id(1)

    @pl.when(c == 0)
    def _():
        xext[:, 0:SUBLANES, :] = jnp.zeros((ns, SUBLANES, CONV_DIM), F32)
        hst[...] = jnp.zeros(hst.shape, F32)

    row = lax.broadcasted_iota(jnp.int32, (L, L), 0)
    col = lax.broadcasted_iota(jnp.int32, (L, L), 1)
    causal = row >= col
    tri = jnp.where(causal, 1.0, 0.0).astype(BF16)
    left = lax.broadcasted_iota(jnp.int32, (L, LANES), 1) < SSD_HEAD_DIM
    a_neg = -jnp.exp(alog_ref[...])
    for s in range(ns):
        ya = _ssd_chunk(xext.at[s], hst.at[s], xbc_ref[s], dt_ref[s], za_ref[s],
                        cw_ref[...], cb_ref[...], dtb_ref[...], a_neg, dexp_ref[...], ng_ref[...],
                        e48_ref[...], ebc_ref[...], tri, causal, left)
        ya_ref[s] = ya.astype(BF16)

    @pl.when(c == pl.num_programs(1) - 1)
    def _():
        for s in range(ns):
            for g in range(SSD_GROUPS):
                hout_ref[s, g * SSD_GW:(g + 1) * SSD_GW, :] = hst[s, g].T
            cout_ref[s] = xext[s, SUBLANES - (CONV_K - 1):SUBLANES, :]


def _ssd_prompt(xbc, dt, za, cw, cb, dtb, alog, dexp, ng, e48, ebc):
    nb, nt, _ = xbc.shape
    L = SSD_CHUNK
    ns = SSD_SEQS_PER_STEP
    blk = lambda w: pl.BlockSpec((ns, L, w), lambda b, c: (b, c, 0))
    per_seq = lambda r, w: pl.BlockSpec((ns, r, w), lambda b, c: (b, 0, 0))
    consts = (cw, cb, dtb, alog, dexp, ng, e48, ebc)
    return pl.pallas_call(
        _ssd_prompt_kernel,
        grid=(nb // ns, nt // L),
        in_specs=[blk(CONV_DIM), blk(LANES), blk(D_A)] + [_resident(a.shape) for a in consts],
        out_specs=[blk(D_A), per_seq(D_A, SSD_STATE), per_seq(CONV_K - 1, CONV_DIM)],
        out_shape=[
            jax.ShapeDtypeStruct((nb, nt, D_A), BF16),
            jax.ShapeDtypeStruct((nb, D_A, SSD_STATE), F32),
            jax.ShapeDtypeStruct((nb, CONV_K - 1, CONV_DIM), F32),
        ],
        scratch_shapes=[
            pltpu.VMEM((ns, L + SUBLANES, CONV_DIM), F32),
            pltpu.VMEM((ns, SSD_GROUPS, SSD_STATE, SSD_GW), F32),
        ],
        compiler_params=_cparams(("arbitrary", "arbitrary")),
        name="ssd_prompt",
    )(xbc, dt, za, *consts)


def _ssd_step_pre_kernel(nb, nt, xbc_ref, cbuf_ref, dt_ref, cw_ref, cb_ref, dtb_ref, alog_ref,
                         dexp_ref, e48_ref,
                         ypart_ref, cm_ref, bm_ref, xw_ref, eax_ref, cd_ref):
    full = [cbuf_ref[k * nb:(k + 1) * nb, :] for k in range(CONV_K - 1)]
    full += [xbc_ref[t * nb:(t + 1) * nb, :] for t in range(nt)]
    e48 = e48_ref[...]
    a_neg = -jnp.exp(alog_ref[...])
    lane = lax.broadcasted_iota(jnp.int32, (nb, LANES), 1)
    first_group = (lane % SSD_HEADS) < SSD_HPG

    xs, bm, cm, dts, acs = [], [], [], [], []
    run = None
    for t in range(nt):
        acc = cb_ref[...] + cw_ref[0:1, :] * full[t]
        for k in range(1, CONV_K):
            acc = acc + cw_ref[k:k + 1, :] * full[t + k]
        xc = _silu(acc)
        xs.append(xc[:, :D_A])
        bm.append(xc[:, D_A:D_A + SSD_BC])
        cm.append(xc[:, D_A + SSD_BC:])
        d = _softplus(dt_ref[t * nb:(t + 1) * nb, :] + dtb_ref[...])
        dts.append(d)
        run = d * a_neg if run is None else run + d * a_neg
        acs.append(run)

    for l in range(nt):
        y = dexp_ref[...] * xs[l]
        for s in range(l + 1):
            cbv = []
            for g in range(SSD_GROUPS):
                sl = slice(g * SSD_STATE, (g + 1) * SSD_STATE)
                cbv.append(jnp.sum(cm[l][:, sl] * bm[s][:, sl], axis=-1, keepdims=True))
            coef = jnp.where(first_group, cbv[0], cbv[1]) * jnp.exp(acs[l] - acs[s]) * dts[s]
            y = y + _spread(coef, e48) * xs[s]
        rows = slice(l * nb, (l + 1) * nb)
        ypart_ref[rows, :] = y
        cm_ref[rows, :] = cm[l]
        bm_ref[rows, :] = bm[l]
        wd = jnp.exp(acs[nt - 1] - acs[l]) * dts[l]
        xw_ref[rows, :] = xs[l] * _spread(wd, e48)
        eax_ref[rows, :] = _spread(jnp.exp(acs[l]), e48)
    cd_ref[...] = jnp.exp(acs[nt - 1])


def _ssd_step_pre(xbc, cbuf, dt, nb, nt, cw, cb, dtb, alog, dexp, e48):
    rows = nb * nt
    args = (xbc, cbuf, dt, cw, cb, dtb, alog, dexp, e48)
    widths = (D_A, SSD_BC, SSD_BC, D_A, D_A)
    return pl.pallas_call(
        functools.partial(_ssd_step_pre_kernel, nb, nt),
        grid=(1,),
        in_specs=[_resident(a.shape) for a in args],
        out_specs=[_whole((rows, w)) for w in widths] + [_whole((nb, LANES))],
        out_shape=[jax.ShapeDtypeStruct((rows, w), F32) for w in widths]
        + [jax.ShapeDtypeStruct((nb, LANES), F32)],
        compiler_params=_cparams(("arbitrary",)),
        name="ssd_step_pre",
    )(*args)


def _ssd_step_state_kernel(nt, ns, cd_ref, cm_ref, bm_ref, xw_ref, eax_ref, ypart_ref, za_ref, ng_ref,
                           h0_ref, ya_ref, hnew_ref):
    b0 = pl.program_id(0) * ns
    for s in range(ns):
        cols = lambda w, g=0, gw=None: slice(s * w + g * (gw or w), s * w + (g + 1) * (gw or w))
        yoff = []
        for g in range(SSD_GROUPS):
            cg = cm_ref[:, cols(SSD_BC, g, SSD_STATE)].astype(BF16)
            bg = bm_ref[:, cols(SSD_BC, g, SSD_STATE)].astype(BF16)
            xwg = xw_ref[:, cols(D_A, g, SSD_GW)].astype(BF16)
            hg = h0_ref[s, g * SSD_GW:(g + 1) * SSD_GW, :]
            yoff.append(lax.dot_general(cg, hg.astype(BF16), (((1,), (1,)), ((), ())),
                                        preferred_element_type=F32))
            upd = lax.dot_general(xwg, bg, (((0,), (0,)), ((), ())),
                                  preferred_element_type=F32)
            for e in range(SSD_HPG):
                r0 = g * SSD_GW + e * SSD_HEAD_DIM
                decay = cd_ref[b0 + s, g * SSD_HPG + e]
                hnew_ref[s, r0:r0 + SSD_HEAD_DIM, :] = (
                    h0_ref[s, r0:r0 + SSD_HEAD_DIM, :] * decay
                    + upd[e * SSD_HEAD_DIM:(e + 1) * SSD_HEAD_DIM, :])
        y = ypart_ref[:, cols(D_A)] + jnp.concatenate(yoff, axis=1) * eax_ref[:, cols(D_A)]
        ya_ref[:, cols(D_A)] = _ssd_gate_norm(y, za_ref[:, cols(D_A)], ng_ref[...]).astype(BF16)


def _ssd_step_state(cd, cm, bm, xw, eax, ypart, za, ng, h0, nb, nt):
    ns = STATE_SEQS_PER_STEP
    blk = lambda w: pl.BlockSpec((nt, ns * w), lambda b: (0, b))
    state = pl.BlockSpec((ns, D_A, SSD_STATE), lambda b: (b, 0, 0))
    return pl.pallas_call(
        functools.partial(_ssd_step_state_kernel, nt, ns),
        grid=(nb // ns,),
        in_specs=[
            pl.BlockSpec(memory_space=pltpu.SMEM),
            blk(SSD_BC), blk(SSD_BC), blk(D_A), blk(D_A), blk(D_A), blk(D_A),
            _resident(ng.shape), state,
        ],
        out_specs=[blk(D_A), state],
        out_shape=[
            jax.ShapeDtypeStruct((nt, nb * D_A), BF16),
            jax.ShapeDtypeStruct((nb, D_A, SSD_STATE), F32),
        ],
        compiler_params=_cparams(("arbitrary",)),
        name="ssd_step_state",
    )(cd, cm, bm, xw, eax, ypart, za, ng, h0)


def _s5_kernel(nb, ts, u_ref, zb_ref, h0re_ref, h0im_ref, are_ref, aim_ref, bs_ref, cs_ref, d_ref,
               gluw_ref, glub_ref,
               yb_ref, hre_ref, him_ref, bu, sre, sim, yacc):
    i = pl.program_id(0)

    @pl.when(i == 0)
    def _():
        sre[...] = h0re_ref[...]
        sim[...] = h0im_ref[...]

    u = u_ref[...]
    ub = u.astype(BF16)
    st = S5_SLAB_ST
    for s in range(S5_SLABS):
        cin = slice(s * S5_SLAB_IN, (s + 1) * S5_SLAB_IN)
        cst = slice(s * st, (s + 1) * st)
        buf = bu.at[s % 2]
        buf[...] = _dot(ub[:, cin], bs_ref[s])
        ar = jnp.broadcast_to(are_ref[:, cst], (SUBLANES, st))
        ai = jnp.broadcast_to(aim_ref[:, cst], (SUBLANES, st))
        for bg in range(nb // SUBLANES):
            rb = slice(bg * SUBLANES, (bg + 1) * SUBLANES)
            xr = sre[rb, cst]
            xi = sim[rb, cst]
            for t in range(ts):
                rt = slice(t * nb + bg * SUBLANES, t * nb + (bg + 1) * SUBLANES)
                nxr = ar * xr - ai * xi + buf[rt, 0:st]
                nxi = ar * xi + ai * xr + buf[rt, st:2 * st]
                buf[rt, 0:st] = nxr
                buf[rt, st:2 * st] = nxi
                xr, xi = nxr, nxi
            sre[rb, cst] = xr
            sim[rb, cst] = xi
        yacc[:, cin] = _dot(buf[...].astype(BF16), cs_ref[s]) + d_ref[:, cin] * u[:, cin]

    g = _gelu_tanh(yacc[...])
    gate = jax.nn.sigmoid(_dot(g.astype(BF16), gluw_ref[...]) + glub_ref[...])
    yb_ref[...] = (g * gate * _silu(zb_ref[...])).astype(BF16)

    @pl.when(i == pl.num_programs(0) - 1)
    def _():
        hre_ref[...] = sre[...]
        him_ref[...] = sim[...]


def _s5(u, zb, h0re, h0im, a_re, a_im, bs, cs, dvec, gluw, glub, nb, nt, ts):
    rows = nb * ts
    blk = pl.BlockSpec((rows, D_B), lambda i: (i, 0))
    consts = (h0re, h0im, a_re, a_im, bs, cs, dvec, gluw, glub)
    return pl.pallas_call(
        functools.partial(_s5_kernel, nb, ts),
        grid=(nt // ts,),
        in_specs=[blk, blk] + [_resident(a.shape) for a in consts],
        out_specs=[blk, _whole((nb, S5_LANES)), _whole((nb, S5_LANES))],
        out_shape=[
            jax.ShapeDtypeStruct((nb * nt, D_B), BF16),
            jax.ShapeDtypeStruct((nb, S5_LANES), F32),
            jax.ShapeDtypeStruct((nb, S5_LANES), F32),
        ],
        scratch_shapes=[
            pltpu.VMEM((2, rows, 2 * S5_SLAB_ST), F32),
            pltpu.VMEM((nb, S5_LANES), F32),
            pltpu.VMEM((nb, S5_LANES), F32),
            pltpu.VMEM((rows, D_B), F32),
        ],
        compiler_params=_cparams(("arbitrary",)),
        name="s5",
    )(u, zb, *consts)


def _outproj_kernel(permute, *refs):
    if permute:
        x_ref, ya_ref, yb_ref, p_ref, perm_ref = refs[:5]
    else:
        x_ref, ya_ref, yb_ref, p_ref = refs[:4]
    woa_ref, wob_ref, gple_ref, wg_ref, wp_ref, gfin_ref, y_ref = refs[-7:]
    nb, tt, _ = x_ref.shape
    rows = nb * tt
    yb = yb_ref[...]
    if permute:
        yb = _dot(perm_ref[...], yb).astype(BF16)
    h = x_ref[...].reshape(rows, D_MODEL) + _dot(ya_ref[...].reshape(rows, D_A), woa_ref[...]) \
        + _dot(yb, wob_ref[...])
    gate = jax.nn.sigmoid(_dot(_rms(h, gple_ref[...]).astype(BF16), wg_ref[...]))
    h = h + _dot(p_ref[...].reshape(rows, PLE_DIM).astype(BF16), wp_ref[...]) * gate
    y_ref[...] = _rms(h, gfin_ref[...]).reshape(nb, tt, D_MODEL)


def _outproj(x3d, ya3d, yb2d, p3d, perm_t, woa, wob, gple, wg, wp, gfin, tt):
    nb, nt, _ = x3d.shape
    rows = nb * tt
    blk3 = lambda w: pl.BlockSpec((nb, tt, w), lambda i: (0, i, 0))
    consts = (woa, wob, gple, wg, wp, gfin)
    args = (x3d, ya3d, yb2d, p3d) + ((perm_t,) if perm_t is not None else ()) + consts
    in_specs = [blk3(D_MODEL), blk3(D_A), pl.BlockSpec((rows, D_B), lambda i: (i, 0)), blk3(PLE_DIM)]
    in_specs += [_resident(perm_t.shape)] if perm_t is not None else []
    in_specs += [_resident(a.shape) for a in consts]
    return pl.pallas_call(
        functools.partial(_outproj_kernel, perm_t is not None),
        grid=(nt // tt,),
        in_specs=in_specs,
        out_specs=blk3(D_MODEL),
        out_shape=jax.ShapeDtypeStruct((nb, nt, D_MODEL), F32),
        compiler_params=_cparams(("arbitrary",)),
        name="outproj",
    )(*args)


def _head_lanes(v):
    rep = jnp.tile(v, (1, HEAD_REPLICAS))
    return jnp.pad(rep, ((0, 0), (0, LANES - rep.shape[-1])))


def _time_major_perm(nb, tt):
    r = jnp.arange(nb * tt)
    src = (r % nb) * tt + r // nb
    return (src[:, None] == jnp.arange(nb * tt)[None, :]).astype(BF16)


def _s5_params(lam_re, lam_im, log_dt, b_re, b_im, c_re, c_im):
    lr = lam_re.astype(F32)
    li = lam_im.astype(F32)
    delta = jnp.exp(log_dt.astype(F32))[:, None]
    mag = jnp.exp(lr * delta)
    ab_re = mag * jnp.cos(li * delta)
    ab_im = mag * jnp.sin(li * delta)
    q_re = ab_re - 1.0
    den = lr * lr + li * li
    f_re = (q_re * lr + ab_im * li) / den
    f_im = (ab_im * lr - q_re * li) / den
    bb_re = f_re[..., None] * b_re.astype(F32) - f_im[..., None] * b_im.astype(F32)
    bb_im = f_re[..., None] * b_im.astype(F32) + f_im[..., None] * b_re.astype(F32)
    gs = S5_GROUPS // S5_SLABS
    eye = jnp.eye(gs, dtype=F32)

    def in_slab(w):
        w = w.reshape(S5_SLABS, gs, S5_STATE, S5_CH)
        return jnp.einsum("sgph,gk->sghkp", w, eye).reshape(S5_SLABS, gs * S5_CH, gs * S5_STATE)

    def out_slab(w):
        w = w.reshape(S5_SLABS, gs, S5_CH, S5_STATE)
        return jnp.einsum("sghp,gk->sgpkh", w, eye).reshape(S5_SLABS, gs * S5_STATE, gs * S5_CH)

    bs = jnp.concatenate([in_slab(bb_re), in_slab(bb_im)], axis=2).astype(BF16)
    cs = jnp.concatenate([out_slab(c_re.astype(F32)), out_slab(-c_im.astype(F32))], axis=1).astype(BF16)
    return ab_re.reshape(1, S5_LANES), ab_im.reshape(1, S5_LANES), bs, cs


def _layer(i, x_prompt, x_sample_tb, p_prompt, p_sample_tb, state_ssd, state_conv, state_s5_re,
           state_s5_im, w_in, g_in, conv_w, conv_b, dt_bias, a_log, ssd_d, ssd_norm_g,
           s5_lambda_re, s5_lambda_im, s5_log_dt, s5_b_re, s5_b_im, s5_c_re, s5_c_im, s5_d,
           glu_w, glu_b, w_out, g_ple, w_ple_gate, w_ple_proj, g_final):
    bp, tp, _ = x_prompt.shape
    ts_, bs_, _ = x_sample_tb.shape

    i1 = D_A
    i2 = i1 + CONV_DIM
    i3 = i2 + SSD_HEADS
    i4 = i3 + D_B
    w = w_in[i].astype(BF16)
    wza, wxbc, wzb, wub = w[:, :i1], w[:, i1:i2], w[:, i3:i4], w[:, i4:]
    wdt = jnp.pad(jnp.tile(w[:, i2:i3], (1, HEAD_REPLICAS)),
                  ((0, 0), (0, LANES - HEAD_REPLICAS * SSD_HEADS)))
    gin = g_in[i].reshape(1, D_MODEL)
    cw = conv_w[i]
    cb = conv_b[i].reshape(1, CONV_DIM)
    dtb = _head_lanes(dt_bias[i].reshape(1, SSD_HEADS))
    alog = _head_lanes(a_log[i].reshape(1, SSD_HEADS))
    dexp = jnp.repeat(ssd_d[i], SSD_HEAD_DIM).reshape(1, D_A)
    ng = ssd_norm_g[i].reshape(1, D_A)
    src = jnp.arange(LANES)
    src_head = jnp.where(src < HEAD_REPLICAS * SSD_HEADS, src % SSD_HEADS, -1)[:, None]
    e48 = (src_head == (jnp.arange(D_A) // SSD_HEAD_DIM)[None, :]).astype(BF16)
    ebc = (src_head == (jnp.arange(SSD_HEADS * LANES) // LANES)[None, :]).astype(BF16)
    a_re, a_im, bs, cs = _s5_params(s5_lambda_re[i], s5_lambda_im[i], s5_log_dt[i],
                                    s5_b_re[i], s5_b_im[i], s5_c_re[i], s5_c_im[i])
    dvec = s5_d[i].reshape(1, D_B)
    gluw = glu_w[i].astype(BF16)
    glub = glu_b[i].reshape(1, D_B)
    wo = w_out[i].astype(BF16)
    woa, wob = wo[:D_A], wo[D_A:]
    gple = g_ple[i].reshape(1, D_MODEL)
    wg = w_ple_gate[i].astype(BF16)
    wp = w_ple_proj[i].astype(BF16)
    gfin = g_final.reshape(1, D_MODEL)

    tt = min(tp, TIME_TILE)
    perm = _time_major_perm(bp, tt)
    za, xbc, dt, zb, ub = _inproj(x_prompt, perm, gin, wza, wxbc, wdt, wzb, wub, tt)
    ya, ssd_p, conv_p = _ssd_prompt(xbc, dt, za, cw, cb, dtb, alog, dexp, ng, e48, ebc)
    zeros_s5 = jnp.zeros((bp, S5_LANES), F32)
    yb, re_p, im_p = _s5(ub, zb, zeros_s5, zeros_s5, a_re, a_im, bs, cs, dvec, gluw, glub, bp, tp, tt)
    hp = _outproj(x_prompt, ya, yb, p_prompt[i], perm.T, woa, wob, gple, wg, wp, gfin, tt)
    out_p = (hp, ssd_p.reshape(bp, SSD_HEADS, SSD_HEAD_DIM, SSD_STATE), conv_p,
             re_p.reshape(bp, S5_GROUPS, S5_STATE), im_p.reshape(bp, S5_GROUPS, S5_STATE))

    rows = ts_ * bs_
    xs3d = x_sample_tb.reshape(1, rows, D_MODEL)
    za, xbc, dt, zb, ub = _inproj(xs3d, None, gin, wza, wxbc, wdt, wzb, wub, rows)
    za, xbc, dt = za[0], xbc[0], dt[0]
    cbuf = jnp.transpose(state_conv[i], (1, 0, 2)).reshape((CONV_K - 1) * bs_, CONV_DIM)
    ypart, cm, bm, xw, eax, cd = _ssd_step_pre(xbc, cbuf, dt, bs_, ts_, cw, cb, dtb, alog, dexp, e48)
    v2 = lambda a: a.reshape(ts_, bs_ * a.shape[-1])
    ya, ssd_s = _ssd_step_state(cd[:, :SSD_HEADS], v2(cm), v2(bm), v2(xw), v2(eax), v2(ypart), v2(za),
                                ng, state_ssd[i].reshape(bs_, D_A, SSD_STATE), bs_, ts_)
    yb, re_s, im_s = _s5(ub, zb, state_s5_re[i].reshape(bs_, S5_LANES),
                         state_s5_im[i].reshape(bs_, S5_LANES),
                         a_re, a_im, bs, cs, dvec, gluw, glub, bs_, ts_, ts_)
    hs = _outproj(xs3d, ya.reshape(1, rows, D_A), yb, p_sample_tb[i].reshape(1, rows, PLE_DIM), None,
                  woa, wob, gple, wg, wp, gfin, rows)
    hist = jnp.concatenate([cbuf.reshape(CONV_K - 1, bs_, CONV_DIM), xbc.reshape(ts_, bs_, CONV_DIM)], axis=0)
    conv_s = jnp.transpose(hist[-(CONV_K - 1):], (1, 0, 2))
    out_s = (hs.reshape(ts_, bs_, D_MODEL),
             ssd_s.reshape(bs_, SSD_HEADS, SSD_HEAD_DIM, SSD_STATE), conv_s,
             re_s.reshape(bs_, S5_GROUPS, S5_STATE), im_s.reshape(bs_, S5_GROUPS, S5_STATE))
    return out_p, out_s


def kernel(x_prompt, x_sample, p_prompt, p_sample, state_ssd, state_conv, state_s5_re, state_s5_im, w_in, g_in, conv_w, conv_b, dt_bias, a_log, ssd_d, ssd_norm_g, s5_lambda_re, s5_lambda_im, s5_log_dt, s5_b_re, s5_b_im, s5_c_re, s5_c_im, s5_d, glu_w, glu_b, w_out, g_ple, w_ple_gate, w_ple_proj, g_final):
    depth = w_in.shape[0]
    assert depth == 1, "the final RMSNorm is fused into the single layer's epilogue"
    xs_tb = jnp.transpose(x_sample, (1, 0, 2))
    ps_tb = jnp.transpose(p_sample, (0, 2, 1, 3))
    out_p, out_s = _layer(0, x_prompt, xs_tb, p_prompt, ps_tb, state_ssd, state_conv, state_s5_re,
                          state_s5_im, w_in, g_in, conv_w, conv_b, dt_bias, a_log, ssd_d, ssd_norm_g,
                          s5_lambda_re, s5_lambda_im, s5_log_dt, s5_b_re, s5_b_im, s5_c_re, s5_c_im,
                          s5_d, glu_w, glu_b, w_out, g_ple, w_ple_gate, w_ple_proj, g_final)
    y_sample = jnp.transpose(out_s[0], (1, 0, 2))
    stack = lambda a: a[None]
    return (out_p[0], y_sample,
            stack(out_p[1]), stack(out_p[2]), stack(out_p[3]), stack(out_p[4]),
            stack(out_s[1]), stack(out_s[2]), stack(out_s[3]), stack(out_s[4]))
```

```python
import functools
import math

import jax
import jax.numpy as jnp
from jax import lax
from jax.experimental import pallas as pl
from jax.experimental.pallas import tpu as pltpu

F32 = jnp.float32
BF16 = jnp.bfloat16

LANES = 128
SUBLANES = 8
VMEM_LIMIT_BYTES = 56 * 1024 * 1024

D_MODEL = 1024
PLE_DIM = 256
D_A = 1024
D_B = 1024
SSD_HEAD_DIM = 64
SSD_HEADS = D_A // SSD_HEAD_DIM
SSD_STATE = 128
SSD_GROUPS = 2
SSD_HPG = SSD_HEADS // SSD_GROUPS
SSD_GW = SSD_HPG * SSD_HEAD_DIM
SSD_BC = SSD_GROUPS * SSD_STATE
CONV_K = 4
CONV_DIM = D_A + 2 * SSD_BC
S5_CH = 16
S5_GROUPS = D_B // S5_CH
S5_STATE = 64
S5_LANES = S5_GROUPS * S5_STATE
S5_SLABS = 4
S5_SLAB_IN = D_B // S5_SLABS
S5_SLAB_ST = S5_LANES // S5_SLABS
EPS = 1e-6

HEAD_REPLICAS = 3

SSD_CHUNK = LANES
SSD_SEQS_PER_STEP = 8
SSD_SEQS_PER_MATMUL = 2
SSD_SCAN_STEPS_PER_PROJ_STEP = 2
TIME_TILE = 64
STATE_SEQS_PER_STEP = 8


def _cparams(semantics):
    return pltpu.CompilerParams(dimension_semantics=semantics, vmem_limit_bytes=VMEM_LIMIT_BYTES)


def _resident(shape):
    nd = len(shape)
    return pl.BlockSpec(shape, lambda *_: (0,) * nd, pipeline_mode=pl.Buffered(1))


def _whole(shape):
    nd = len(shape)
    return pl.BlockSpec(shape, lambda *_: (0,) * nd)


def _silu(v):
    return v * jax.nn.sigmoid(v)


def _softplus(v):
    return jnp.maximum(v, 0.0) + jnp.log1p(jnp.exp(-jnp.abs(v)))


def _gelu_tanh(v):
    c = math.sqrt(2.0 / math.pi)
    return 0.5 * v * (1.0 + jnp.tanh(c * (v + 0.044715 * (v * v * v))))


def _rms(v, g):
    ms = jnp.mean(v * v, axis=-1, keepdims=True)
    return v * lax.rsqrt(ms + EPS) * g


def _dot(a, b):
    return jnp.dot(a, b, preferred_element_type=F32)


def _dot_nt(a, b):
    return lax.dot_general(a, b, (((1,), (1,)), ((), ())), preferred_element_type=F32)


def _split3(v):
    hi = v.astype(BF16).astype(F32)
    r1 = v - hi
    mid = r1.astype(BF16).astype(F32)
    lo = r1 - mid
    return hi, mid, lo


def _cumsum_rows(tri, v):
    hi, mid, lo = _split3(v)
    return _dot(tri, hi.astype(BF16)) + _dot(tri, mid.astype(BF16)) + _dot(tri, lo.astype(BF16))


def _pack3(v):
    hi, mid, lo = _split3(v)
    lane = lax.broadcasted_iota(jnp.int32, v.shape, 1)
    return jnp.where(lane < SSD_HEADS, hi, jnp.where(lane < 2 * SSD_HEADS, mid, lo)).astype(BF16)


def _spread(v, sel):
    return _dot(_pack3(v), sel)


def _inproj_s5_kernel(x_ref, g_ref, perm_ref, w_ref, zb_ref, ub_ref):
    nb, tt, _ = x_ref.shape
    n = _rms(x_ref[...].reshape(nb * tt, D_MODEL), g_ref[...]).astype(BF16)
    n_tb = _dot(perm_ref[...], n).astype(BF16)
    zb_ref[...] = _dot(n_tb, w_ref[:, :D_B])
    ub_ref[...] = _dot(n_tb, w_ref[:, D_B:])


def _inproj_s5(x3d, perm, g_in, w_s5, tt):
    nb, nt, _ = x3d.shape
    blk2 = pl.BlockSpec((nb * tt, D_B), lambda i: (i, 0))
    return pl.pallas_call(
        _inproj_s5_kernel,
        grid=(nt // tt,),
        in_specs=[pl.BlockSpec((nb, tt, D_MODEL), lambda i: (0, i, 0)),
                  _resident(g_in.shape), _resident(perm.shape), _resident(w_s5.shape)],
        out_specs=[blk2, blk2],
        out_shape=[jax.ShapeDtypeStruct((nb * nt, D_B), F32)] * 2,
        compiler_params=_cparams(("arbitrary",)),
        name="inproj_s5",
    )(x3d, g_in, perm, w_s5)


def _inproj_rows_kernel(x_ref, g_ref, ws5_ref, wssd_ref, za_ref, xbc_ref, dt_ref, zb_ref, ub_ref):
    n = _rms(x_ref[...], g_ref[...]).astype(BF16)
    pa = _dot(n, wssd_ref[...])
    za_ref[...] = pa[:, :D_A]
    xbc_ref[...] = pa[:, D_A:D_A + CONV_DIM]
    dt_ref[...] = pa[:, D_A + CONV_DIM:]
    pb = _dot(n, ws5_ref[...])
    zb_ref[...] = pb[:, :D_B]
    ub_ref[...] = pb[:, D_B:]


def _inproj_rows(x2d, g_in, w_s5, w_ssd):
    rows = x2d.shape[0]
    widths = (D_A, CONV_DIM, LANES, D_B, D_B)
    args = (x2d, g_in, w_s5, w_ssd)
    return pl.pallas_call(
        _inproj_rows_kernel,
        grid=(1,),
        in_specs=[_resident(a.shape) for a in args],
        out_specs=[_whole((rows, w)) for w in widths],
        out_shape=[jax.ShapeDtypeStruct((rows, w), F32) for w in widths],
        compiler_params=_cparams(("arbitrary",)),
        name="inproj_rows",
    )(*args)


def _ssd_gate_norm(y, za, ng):
    y = y * _silu(za)
    parts = []
    for g in range(SSD_GROUPS):
        sl = slice(g * SSD_GW, (g + 1) * SSD_GW)
        parts.append(_rms(y[:, sl], ng[:, sl]))
    return jnp.concatenate(parts, axis=-1)


def _ssd_chunk_steps(xext, hst, proj, rows, out_ref, cw, cb, dtb, a_neg, dexp, ng, e48, ebc,
                     tri, causal, left):
    L = SSD_CHUNK
    xext[SUBLANES:SUBLANES + L, :] = proj["xbc"][rows]
    ext = xext[...]
    acc = cw[0:1, :] * ext
    for k in range(1, CONV_K):
        acc = cw[k:k + 1, :] * ext + pltpu.roll(acc, 1, axis=0)
        yield
    xc = _silu(acc[SUBLANES:, :] + cb)
    xext[0:SUBLANES, :] = ext[L:L + SUBLANES, :]
    yield

    xs = xc[:, :D_A]
    bm = xc[:, D_A:D_A + SSD_BC]
    cm = xc[:, D_A + SSD_BC:]

    dt = _softplus(proj["dt"][rows] + dtb)
    acs = _cumsum_rows(tri, dt * a_neg)
    ea = jnp.exp(acs)
    wd = jnp.exp(acs[L - 1:L, :] - acs) * dt
    dt_x = _spread(dt, e48)
    wd_x = _spread(wd, e48)
    ea_x = _spread(ea, e48)
    acs_b = _spread(acs, ebc)
    acs_t = acs.T
    yield

    xdt = xs * dt_x
    xw = (xs * wd_x).astype(BF16)
    yield

    ydiag = []
    yoff = []
    for g in range(SSD_GROUPS):
        cg = cm[:, g * SSD_STATE:(g + 1) * SSD_STATE].astype(BF16)
        bg32 = bm[:, g * SSD_STATE:(g + 1) * SSD_STATE]
        bg = bg32.astype(BF16)
        cbm = _dot_nt(cg, bg)
        for j in range(SSD_HPG // 2):
            q = g * (SSD_HPG // 2) + j
            pair = xdt[:, q * LANES:(q + 1) * LANES]
            rhs = [jnp.where(left, pair, 0.0).astype(BF16), jnp.where(left, 0.0, pair).astype(BF16)]
            ms = []
            for i in range(2):
                h = 2 * q + i
                seg = acs_b[:, h * LANES:(h + 1) * LANES] - acs_t[h:h + 1, :]
                dec = jnp.exp(jnp.where(causal, seg, -jnp.inf))
                ms.append((cbm * dec).astype(BF16))
            ydiag.append(_dot(jnp.concatenate(ms, axis=1), jnp.concatenate(rhs, axis=0)))
            yield
        hg = hst[g]
        yoff.append(_dot(cg, hg.astype(BF16)))
        cd = ea_x[L - 1:L, g * SSD_GW:(g + 1) * SSD_GW]
        hst[g] = hg * cd + _dot(bg32.T.astype(BF16), xw[:, g * SSD_GW:(g + 1) * SSD_GW])
        yield

    y = jnp.concatenate(ydiag, axis=1) + jnp.concatenate(yoff, axis=1) * ea_x + dexp * xs
    yield
    out_ref[...] = _ssd_gate_norm(y, proj["za"][rows], ng).astype(out_ref.dtype)


def _project_steps(x_ref, gin, w_ref, proj):
    rows = x_ref.shape[0] * x_ref.shape[1]
    n = _rms(x_ref[...].reshape(rows, D_MODEL), gin).astype(BF16)
    yield
    tile = 2 * LANES
    for name, lo, hi in (("xbc", D_A, D_A + CONV_DIM), ("dt", D_A + CONV_DIM, w_ref.shape[1]),
                         ("za", 0, D_A)):
        parts = []
        for c0 in range(lo, hi, tile):
            parts.append(_dot(n, w_ref[:, c0:min(c0 + tile, hi)]))
            yield
        proj[name] = parts[0] if len(parts) == 1 else jnp.concatenate(parts, axis=1)


def _interleave(main, filler, ratio):
    k = 0
    for _ in main:
        k += 1
        if filler is not None and k % ratio == 0:
            next(filler, None)
    if filler is not None:
        for _ in filler:
            pass


def _round_robin(*gens):
    live = list(gens)
    while live:
        for g in list(live):
            try:
                next(g)
                yield
            except StopIteration:
                live.remove(g)


def _ssd_prompt_kernel(x_ref, gin_ref, w_ref, cw_ref, cb_ref, dtb_ref, alog_ref,
                       dexp_ref, ng_ref, e48_ref, ebc_ref,
                       ya_ref, hout_ref, cout_ref, xext, hst):
    L = SSD_CHUNK
    ns = x_ref.shape[0]
    c = pl.program_id(1)

    @pl.when(c == 0)
    def _():
        xext[:, 0:SUBLANES, :] = jnp.zeros((ns, SUBLANES, CONV_DIM), F32)
        hst[...] = jnp.zeros(hst.shape, F32)

    row = lax.broadcasted_iota(jnp.int32, (L, L), 0)
    col = lax.broadcasted_iota(jnp.int32, (L, L), 1)
    causal = row >= col
    tri = jnp.where(causal, 1.0, 0.0).astype(BF16)
    left = lax.broadcasted_iota(jnp.int32, (L, LANES), 1) < SSD_HEAD_DIM
    a_neg = -jnp.exp(alog_ref[...])

    gsz = SSD_SEQS_PER_MATMUL
    groups = list(range(0, ns, gsz))
    projs = [dict() for _ in groups]
    project = lambda k: _project_steps(x_ref.at[groups[k]:groups[k] + gsz], gin_ref[...], w_ref, projs[k])
    _interleave(project(0), None, 1)
    for k, s0 in enumerate(groups):
        scans = _round_robin(*[
            _ssd_chunk_steps(xext.at[s0 + i], hst.at[s0 + i], projs[k], slice(i * L, (i + 1) * L),
                             ya_ref.at[s0 + i], cw_ref[...], cb_ref[...], dtb_ref[...], a_neg,
                             dexp_ref[...], ng_ref[...], e48_ref[...], ebc_ref[...], tri, causal, left)
            for i in range(gsz)])
        _interleave(scans, project(k + 1) if k + 1 < len(groups) else None, SSD_SCAN_STEPS_PER_PROJ_STEP)

    @pl.when(c == pl.num_programs(1) - 1)
    def _():
        for s in range(ns):
            for g in range(SSD_GROUPS):
                hout_ref[s, g * SSD_GW:(g + 1) * SSD_GW, :] = hst[s, g].T
            cout_ref[s] = xext[s, SUBLANES - (CONV_K - 1):SUBLANES, :]


def _ssd_prompt(x3d, g_in, w_ssd, cw, cb, dtb, alog, dexp, ng, e48, ebc):
    nb, nt, _ = x3d.shape
    L = SSD_CHUNK
    ns = min(nb, SSD_SEQS_PER_STEP)
    blk = lambda w: pl.BlockSpec((ns, L, w), lambda b, c: (b, c, 0))
    per_seq = lambda r, w: pl.BlockSpec((ns, r, w), lambda b, c: (b, 0, 0))
    consts = (g_in, w_ssd, cw, cb, dtb, alog, dexp, ng, e48, ebc)
    return pl.pallas_call(
        _ssd_prompt_kernel,
        grid=(nb // ns, nt // L),
        in_specs=[blk(D_MODEL)] + [_resident(a.shape) for a in consts],
        out_specs=[blk(D_A), per_seq(D_A, SSD_STATE), per_seq(CONV_K - 1, CONV_DIM)],
        out_shape=[
            jax.ShapeDtypeStruct((nb, nt, D_A), BF16),
            jax.ShapeDtypeStruct((nb, D_A, SSD_STATE), F32),
            jax.ShapeDtypeStruct((nb, CONV_K - 1, CONV_DIM), F32),
        ],
        scratch_shapes=[
            pltpu.VMEM((ns, L + SUBLANES, CONV_DIM), F32),
            pltpu.VMEM((ns, SSD_GROUPS, SSD_STATE, SSD_GW), F32),
        ],
        compiler_params=_cparams(("arbitrary", "arbitrary")),
        name="ssd_prompt",
    )(x3d, *consts)


def _ssd_step_pre_kernel(nb, nt, xbc_ref, cbuf_ref, dt_ref, cw_ref, cb_ref, dtb_ref, alog_ref,
                         dexp_ref, e48_ref,
                         ypart_ref, cm_ref, bm_ref, xw_ref, eax_ref, cd_ref):
    full = [cbuf_ref[k * nb:(k + 1) * nb, :] for k in range(CONV_K - 1)]
    full += [xbc_ref[t * nb:(t + 1) * nb, :] for t in range(nt)]
    e48 = e48_ref[...]
    a_neg = -jnp.exp(alog_ref[...])
    lane = lax.broadcasted_iota(jnp.int32, (nb, LANES), 1)
    first_group = (lane % SSD_HEADS) < SSD_HPG

    xs, bm, cm, dts, acs = [], [], [], [], []
    run = None
    for t in range(nt):
        acc = cb_ref[...] + cw_ref[0:1, :] * full[t]
        for k in range(1, CONV_K):
            acc = acc + cw_ref[k:k + 1, :] * full[t + k]
        xc = _silu(acc)
        xs.append(xc[:, :D_A])
        bm.append(xc[:, D_A:D_A + SSD_BC])
        cm.append(xc[:, D_A + SSD_BC:])
        d = _softplus(dt_ref[t * nb:(t + 1) * nb, :] + dtb_ref[...])
        dts.append(d)
        run = d * a_neg if run is None else run + d * a_neg
        acs.append(run)

    for l in range(nt):
        y = dexp_ref[...] * xs[l]
        for s in range(l + 1):
            cbv = []
            for g in range(SSD_GROUPS):
                sl = slice(g * SSD_STATE, (g + 1) * SSD_STATE)
                cbv.append(jnp.sum(cm[l][:, sl] * bm[s][:, sl], axis=-1, keepdims=True))
            coef = jnp.where(first_group, cbv[0], cbv[1]) * jnp.exp(acs[l] - acs[s]) * dts[s]
            y = y + _spread(coef, e48) * xs[s]
        rows = slice(l * nb, (l + 1) * nb)
        ypart_ref[rows, :] = y
        cm_ref[rows, :] = cm[l]
        bm_ref[rows, :] = bm[l]
        wd = jnp.exp(acs[nt - 1] - acs[l]) * dts[l]
        xw_ref[rows, :] = xs[l] * _spread(wd, e48)
        eax_ref[rows, :] = _spread(jnp.exp(acs[l]), e48)
    cd_ref[...] = jnp.exp(acs[nt - 1])


def _ssd_step_pre(xbc, cbuf, dt, nb, nt, cw, cb, dtb, alog, dexp, e48):
    rows = nb * nt
    args = (xbc, cbuf, dt, cw, cb, dtb, alog, dexp, e48)
    widths = (D_A, SSD_BC, SSD_BC, D_A, D_A)
    return pl.pallas_call(
        functools.partial(_ssd_step_pre_kernel, nb, nt),
        grid=(1,),
        in_specs=[_resident(a.shape) for a in args],
        out_specs=[_whole((rows, w)) for w in widths] + [_whole((nb, LANES))],
        out_shape=[jax.ShapeDtypeStruct((rows, w), F32) for w in widths]
        + [jax.ShapeDtypeStruct((nb, LANES), F32)],
        compiler_params=_cparams(("arbitrary",)),
        name="ssd_step_pre",
    )(*args)


def _ssd_step_state_kernel(nt, ns, cd_ref, cm_ref, bm_ref, xw_ref, eax_ref, ypart_ref, za_ref, ng_ref,
                           h0_ref, ya_ref, hnew_ref):
    b0 = pl.program_id(0) * ns
    for s in range(ns):
        cols = lambda w, g=0, gw=None: slice(s * w + g * (gw or w), s * w + (g + 1) * (gw or w))
        yoff = []
        for g in range(SSD_GROUPS):
            cg = cm_ref[:, cols(SSD_BC, g, SSD_STATE)].astype(BF16)
            bg = bm_ref[:, cols(SSD_BC, g, SSD_STATE)].astype(BF16)
            xwg = xw_ref[:, cols(D_A, g, SSD_GW)].astype(BF16)
            hg = h0_ref[s, g * SSD_GW:(g + 1) * SSD_GW, :]
            yoff.append(lax.dot_general(cg, hg.astype(BF16), (((1,), (1,)), ((), ())),
                                        preferred_element_type=F32))
            upd = lax.dot_general(xwg, bg, (((0,), (0,)), ((), ())),
                                  preferred_element_type=F32)
            for e in range(SSD_HPG):
                r0 = g * SSD_GW + e * SSD_HEAD_DIM
                decay = cd_ref[b0 + s, g * SSD_HPG + e]
                hnew_ref[s, r0:r0 + SSD_HEAD_DIM, :] = (
                    h0_ref[s, r0:r0 + SSD_HEAD_DIM, :] * decay
                    + upd[e * SSD_HEAD_DIM:(e + 1) * SSD_HEAD_DIM, :])
        y = ypart_ref[:, cols(D_A)] + jnp.concatenate(yoff, axis=1) * eax_ref[:, cols(D_A)]
        ya_ref[:, cols(D_A)] = _ssd_gate_norm(y, za_ref[:, cols(D_A)], ng_ref[...]).astype(BF16)


def _ssd_step_state(cd, cm, bm, xw, eax, ypart, za, ng, h0, nb, nt):
    ns = STATE_SEQS_PER_STEP
    blk = lambda w: pl.BlockSpec((nt, ns * w), lambda b: (0, b))
    state = pl.BlockSpec((ns, D_A, SSD_STATE), lambda b: (b, 0, 0))
    return pl.pallas_call(
        functools.partial(_ssd_step_state_kernel, nt, ns),
        grid=(nb // ns,),
        in_specs=[
            pl.BlockSpec(memory_space=pltpu.SMEM),
            blk(SSD_BC), blk(SSD_BC), blk(D_A), blk(D_A), blk(D_A), blk(D_A),
            _resident(ng.shape), state,
        ],
        out_specs=[blk(D_A), state],
        out_shape=[
            jax.ShapeDtypeStruct((nt, nb * D_A), BF16),
            jax.ShapeDtypeStruct((nb, D_A, SSD_STATE), F32),
        ],
        compiler_params=_cparams(("arbitrary",)),
        name="ssd_step_state",
    )(cd, cm, bm, xw, eax, ypart, za, ng, h0)


def _s5_kernel(nb, ts, u_ref, zb_ref, h0re_ref, h0im_ref, are_ref, aim_ref, bs_ref, cs_ref, d_ref,
               gluw_ref, glub_ref,
               yb_ref, hre_ref, him_ref, bu, sre, sim, yacc):
    i = pl.program_id(0)

    @pl.when(i == 0)
    def _():
        sre[...] = h0re_ref[...]
        sim[...] = h0im_ref[...]

    u = u_ref[...]
    ub = u.astype(BF16)
    st = S5_SLAB_ST
    for s in range(S5_SLABS):
        cin = slice(s * S5_SLAB_IN, (s + 1) * S5_SLAB_IN)
        cst = slice(s * st, (s + 1) * st)
        buf = bu.at[s % 2]
        buf[...] = _dot_nt(ub[:, cin], bs_ref[s])
        ar = jnp.broadcast_to(are_ref[:, cst], (SUBLANES, st))
        ai = jnp.broadcast_to(aim_ref[:, cst], (SUBLANES, st))
        for bg in range(nb // SUBLANES):
            rb = slice(bg * SUBLANES, (bg + 1) * SUBLANES)
            xr = sre[rb, cst]
            xi = sim[rb, cst]
            for t in range(ts):
                rt = slice(t * nb + bg * SUBLANES, t * nb + (bg + 1) * SUBLANES)
                nxr = ar * xr - ai * xi + buf[rt, 0:st]
                nxi = ar * xi + ai * xr + buf[rt, st:2 * st]
                buf[rt, 0:st] = nxr
                buf[rt, st:2 * st] = nxi
                xr, xi = nxr, nxi
            sre[rb, cst] = xr
            sim[rb, cst] = xi
        yacc[:, cin] = _dot_nt(buf[...].astype(BF16), cs_ref[s]) + d_ref[:, cin] * u[:, cin]

    g = _gelu_tanh(yacc[...])
    gate = jax.nn.sigmoid(_dot(g.astype(BF16), gluw_ref[...]) + glub_ref[...])
    yb_ref[...] = (g * gate * _silu(zb_ref[...])).astype(BF16)

    @pl.when(i == pl.num_programs(0) - 1)
    def _():
        hre_ref[...] = sre[...]
        him_ref[...] = sim[...]


def _s5(u, zb, h0re, h0im, a_re, a_im, bs, cs, dvec, gluw, glub, nb, nt, ts):
    rows = nb * ts
    blk = pl.BlockSpec((rows, D_B), lambda i: (i, 0))
    consts = (h0re, h0im, a_re, a_im, bs, cs, dvec, gluw, glub)
    return pl.pallas_call(
        functools.partial(_s5_kernel, nb, ts),
        grid=(nt // ts,),
        in_specs=[blk, blk] + [_resident(a.shape) for a in consts],
        out_specs=[blk, _whole((nb, S5_LANES)), _whole((nb, S5_LANES))],
        out_shape=[
            jax.ShapeDtypeStruct((nb * nt, D_B), BF16),
            jax.ShapeDtypeStruct((nb, S5_LANES), F32),
            jax.ShapeDtypeStruct((nb, S5_LANES), F32),
        ],
        scratch_shapes=[
            pltpu.VMEM((2, rows, 2 * S5_SLAB_ST), F32),
            pltpu.VMEM((nb, S5_LANES), F32),
            pltpu.VMEM((nb, S5_LANES), F32),
            pltpu.VMEM((rows, D_B), F32),
        ],
        compiler_params=_cparams(("arbitrary",)),
        name="s5",
    )(u, zb, *consts)


def _outproj_kernel(permute, *refs):
    if permute:
        x_ref, ya_ref, yb_ref, p_ref, perm_ref = refs[:5]
    else:
        x_ref, ya_ref, yb_ref, p_ref = refs[:4]
    woa_ref, wob_ref, gple_ref, wg_ref, wp_ref, gfin_ref, y_ref = refs[-7:]
    nb, tt, _ = x_ref.shape
    rows = nb * tt
    yb = yb_ref[...]
    if permute:
        yb = _dot(perm_ref[...], yb).astype(BF16)
    h = x_ref[...].reshape(rows, D_MODEL) + _dot(ya_ref[...].reshape(rows, D_A), woa_ref[...]) \
        + _dot(yb, wob_ref[...])
    gate = jax.nn.sigmoid(_dot(_rms(h, gple_ref[...]).astype(BF16), wg_ref[...]))
    h = h + _dot(p_ref[...].reshape(rows, PLE_DIM).astype(BF16), wp_ref[...]) * gate
    y_ref[...] = _rms(h, gfin_ref[...]).reshape(nb, tt, D_MODEL)


def _outproj(x3d, ya3d, yb2d, p3d, perm_t, woa, wob, gple, wg, wp, gfin, tt):
    nb, nt, _ = x3d.shape
    rows = nb * tt
    blk3 = lambda w: pl.BlockSpec((nb, tt, w), lambda i: (0, i, 0))
    consts = (woa, wob, gple, wg, wp, gfin)
    args = (x3d, ya3d, yb2d, p3d) + ((perm_t,) if perm_t is not None else ()) + consts
    in_specs = [blk3(D_MODEL), blk3(D_A), pl.BlockSpec((rows, D_B), lambda i: (i, 0)), blk3(PLE_DIM)]
    in_specs += [_resident(perm_t.shape)] if perm_t is not None else []
    in_specs += [_resident(a.shape) for a in consts]
    return pl.pallas_call(
        functools.partial(_outproj_kernel, perm_t is not None),
        grid=(nt // tt,),
        in_specs=in_specs,
        out_specs=blk3(D_MODEL),
        out_shape=jax.ShapeDtypeStruct((nb, nt, D_MODEL), F32),
        compiler_params=_cparams(("arbitrary",)),
        name="outproj",
    )(*args)


def _head_lanes(v):
    rep = jnp.tile(v, (1, HEAD_REPLICAS))
    return jnp.pad(rep, ((0, 0), (0, LANES - rep.shape[-1])))


def _time_major_perm(nb, tt):
    r = jnp.arange(nb * tt)
    src = (r % nb) * tt + r // nb
    return (src[:, None] == jnp.arange(nb * tt)[None, :]).astype(BF16)


def _s5_params(lam_re, lam_im, log_dt, b_re, b_im, c_re, c_im):
    lr = lam_re.astype(F32)
    li = lam_im.astype(F32)
    delta = jnp.exp(log_dt.astype(F32))[:, None]
    mag = jnp.exp(lr * delta)
    ab_re = mag * jnp.cos(li * delta)
    ab_im = mag * jnp.sin(li * delta)
    q_re = ab_re - 1.0
    den = lr * lr + li * li
    f_re = (q_re * lr + ab_im * li) / den
    f_im = (ab_im * lr - q_re * li) / den
    bb_re = f_re[..., None] * b_re.astype(F32) - f_im[..., None] * b_im.astype(F32)
    bb_im = f_re[..., None] * b_im.astype(F32) + f_im[..., None] * b_re.astype(F32)
    gs = S5_GROUPS // S5_SLABS
    eye = jnp.eye(gs, dtype=F32)

    def slab_diag(w):
        a, b = w.shape[1:]
        w = w.astype(F32).reshape(S5_SLABS, gs, a, 1, b) * eye[None, :, None, :, None]
        return w.reshape(S5_SLABS, gs * a, gs * b)

    bs = jnp.concatenate([slab_diag(bb_re), slab_diag(bb_im)], axis=1).astype(BF16)
    cs = jnp.concatenate([slab_diag(c_re), slab_diag(-c_im)], axis=2).astype(BF16)
    return ab_re.reshape(1, S5_LANES), ab_im.reshape(1, S5_LANES), bs, cs


def _layer(i, x_prompt, x_sample_tb, p_prompt, p_sample_tb, state_ssd, state_conv, state_s5_re,
           state_s5_im, w_in, g_in, conv_w, conv_b, dt_bias, a_log, ssd_d, ssd_norm_g,
           s5_lambda_re, s5_lambda_im, s5_log_dt, s5_b_re, s5_b_im, s5_c_re, s5_c_im, s5_d,
           glu_w, glu_b, w_out, g_ple, w_ple_gate, w_ple_proj, g_final):
    bp, tp, _ = x_prompt.shape
    ts_, bs_, _ = x_sample_tb.shape

    i1 = D_A
    i2 = i1 + CONV_DIM
    i3 = i2 + SSD_HEADS
    i4 = i3 + D_B
    w = w_in[i]
    w_s5 = w[:, i3:].astype(BF16)
    pad_dt = jnp.zeros((D_MODEL, LANES - HEAD_REPLICAS * SSD_HEADS), F32)
    w_ssd = jnp.concatenate([w[:, :i2]] + [w[:, i2:i3]] * HEAD_REPLICAS + [pad_dt],
                            axis=1).astype(BF16)
    gin = g_in[i].reshape(1, D_MODEL)
    cw = conv_w[i]
    cb = conv_b[i].reshape(1, CONV_DIM)
    dtb = _head_lanes(dt_bias[i].reshape(1, SSD_HEADS))
    alog = _head_lanes(a_log[i].reshape(1, SSD_HEADS))
    dexp = jnp.repeat(ssd_d[i], SSD_HEAD_DIM).reshape(1, D_A)
    ng = ssd_norm_g[i].reshape(1, D_A)
    src = jnp.arange(LANES)
    src_head = jnp.where(src < HEAD_REPLICAS * SSD_HEADS, src % SSD_HEADS, -1)[:, None]
    e48 = (src_head == (jnp.arange(D_A) // SSD_HEAD_DIM)[None, :]).astype(BF16)
    ebc = (src_head == (jnp.arange(SSD_HEADS * LANES) // LANES)[None, :]).astype(BF16)
    a_re, a_im, bs, cs = _s5_params(s5_lambda_re[i], s5_lambda_im[i], s5_log_dt[i],
                                    s5_b_re[i], s5_b_im[i], s5_c_re[i], s5_c_im[i])
    dvec = s5_d[i].reshape(1, D_B)
    gluw = glu_w[i].astype(BF16)
    glub = glu_b[i].reshape(1, D_B)
    wo = w_out[i].astype(BF16)
    woa, wob = wo[:D_A], wo[D_A:]
    gple = g_ple[i].reshape(1, D_MODEL)
    wg = w_ple_gate[i].astype(BF16)
    wp = w_ple_proj[i].astype(BF16)
    gfin = g_final.reshape(1, D_MODEL)

    tt = min(tp, TIME_TILE)
    perm = _time_major_perm(bp, tt)
    zb, ub = _inproj_s5(x_prompt, perm, gin, w_s5, tt)
    ya, ssd_p, conv_p = _ssd_prompt(x_prompt, gin, w_ssd, cw, cb, dtb, alog, dexp, ng, e48, ebc)
    zeros_s5 = jnp.zeros((bp, S5_LANES), F32)
    yb, re_p, im_p = _s5(ub, zb, zeros_s5, zeros_s5, a_re, a_im, bs, cs, dvec, gluw, glub, bp, tp, tt)
    hp = _outproj(x_prompt, ya, yb, p_prompt[i], perm.T, woa, wob, gple, wg, wp, gfin, tt)
    out_p = (hp, ssd_p.reshape(bp, SSD_HEADS, SSD_HEAD_DIM, SSD_STATE), conv_p,
             re_p.reshape(bp, S5_GROUPS, S5_STATE), im_p.reshape(bp, S5_GROUPS, S5_STATE))

    rows = ts_ * bs_
    xs3d = x_sample_tb.reshape(1, rows, D_MODEL)
    za, xbc, dt, zb, ub = _inproj_rows(xs3d[0], gin, w_s5, w_ssd)
    cbuf = jnp.transpose(state_conv[i], (1, 0, 2)).reshape((CONV_K - 1) * bs_, CONV_DIM)
    ypart, cm, bm, xw, eax, cd = _ssd_step_pre(xbc, cbuf, dt, bs_, ts_, cw, cb, dtb, alog, dexp, e48)
    v2 = lambda a: a.reshape(ts_, bs_ * a.shape[-1])
    ya, ssd_s = _ssd_step_state(cd[:, :SSD_HEADS], v2(cm), v2(bm), v2(xw), v2(eax), v2(ypart), v2(za),
                                ng, state_ssd[i].reshape(bs_, D_A, SSD_STATE), bs_, ts_)
    yb, re_s, im_s = _s5(ub, zb, state_s5_re[i].reshape(bs_, S5_LANES),
                         state_s5_im[i].reshape(bs_, S5_LANES),
                         a_re, a_im, bs, cs, dvec, gluw, glub, bs_, ts_, ts_)
    hs = _outproj(xs3d, ya.reshape(1, rows, D_A), yb, p_sample_tb[i].reshape(1, rows, PLE_DIM), None,
                  woa, wob, gple, wg, wp, gfin, rows)
    hist = jnp.concatenate([cbuf.reshape(CONV_K - 1, bs_, CONV_DIM), xbc.reshape(ts_, bs_, CONV_DIM)], axis=0)
    conv_s = jnp.transpose(hist[-(CONV_K - 1):], (1, 0, 2))
    out_s = (hs.reshape(ts_, bs_, D_MODEL),
             ssd_s.reshape(bs_, SSD_HEADS, SSD_HEAD_DIM, SSD_STATE), conv_s,
             re_s.reshape(bs_, S5_GROUPS, S5_STATE), im_s.reshape(bs_, S5_GROUPS, S5_STATE))
    return out_p, out_s


def kernel(x_prompt, x_sample, p_prompt, p_sample, state_ssd, state_conv, state_s5_re, state_s5_im, w_in, g_in, conv_w, conv_b, dt_bias, a_log, ssd_d, ssd_norm_g, s5_lambda_re, s5_lambda_im, s5_log_dt, s5_b_re, s5_b_im, s5_c_re, s5_c_im, s5_d, glu_w, glu_b, w_out, g_ple, w_ple_gate, w_ple_proj, g_final):
    depth = w_in.shape[0]
    assert depth == 1, "the final RMSNorm is fused into the single layer's epilogue"
    xs_tb = jnp.transpose(x_sample, (1, 0, 2))
    ps_tb = jnp.transpose(p_sample, (0, 2, 1, 3))
    out_p, out_s = _layer(0, x_prompt, xs_tb, p_prompt, ps_tb, state_ssd, state_conv, state_s5_re,
                          state_s5_im, w_in, g_in, conv_w, conv_b, dt_bias, a_log, ssd_d, ssd_norm_g,
                          s5_lambda_re, s5_lambda_im, s5_log_dt, s5_b_re, s5_b_im, s5_c_re, s5_c_im,
                          s5_d, glu_w, glu_b, w_out, g_ple, w_ple_gate, w_ple_proj, g_final)
    y_sample = jnp.transpose(out_s[0], (1, 0, 2))
    stack = lambda a: a[None]
    return (out_p[0], y_sample,
            stack(out_p[1]), stack(out_p[2]), stack(out_p[3]), stack(out_p[4]),
            stack(out_s[1]), stack(out_s[2]), stack(out_s[3]), stack(out_s[4]))
```

```python
import functools
import math

import jax
import jax.numpy as jnp
from jax import lax
from jax.experimental import pallas as pl
from jax.experimental.pallas import tpu as pltpu

F32 = jnp.float32
BF16 = jnp.bfloat16

LANES = 128
SUBLANES = 8
VMEM_LIMIT_BYTES = 56 * 1024 * 1024

D_MODEL = 1024
PLE_DIM = 256
D_A = 1024
D_B = 1024
SSD_HEAD_DIM = 64
SSD_HEADS = D_A // SSD_HEAD_DIM
SSD_STATE = 128
SSD_GROUPS = 2
SSD_HPG = SSD_HEADS // SSD_GROUPS
SSD_GW = SSD_HPG * SSD_HEAD_DIM
SSD_BC = SSD_GROUPS * SSD_STATE
CONV_K = 4
CONV_DIM = D_A + 2 * SSD_BC
S5_CH = 16
S5_GROUPS = D_B // S5_CH
S5_STATE = 64
S5_LANES = S5_GROUPS * S5_STATE
S5_SLABS = 4
S5_SLAB_IN = D_B // S5_SLABS
S5_SLAB_ST = S5_LANES // S5_SLABS
EPS = 1e-6

HEAD_REPLICAS = 3

SSD_CHUNK = LANES
SSD_SEQS_PER_STEP = 8
SSD_SEQS_PER_MATMUL = 2
SSD_SCAN_STEPS_PER_PROJ_STEP = 2
TIME_TILE = 64
STATE_SEQS_PER_STEP = 8


def _cparams(semantics):
    return pltpu.CompilerParams(dimension_semantics=semantics, vmem_limit_bytes=VMEM_LIMIT_BYTES)


def _resident(shape):
    nd = len(shape)
    return pl.BlockSpec(shape, lambda *_: (0,) * nd, pipeline_mode=pl.Buffered(1))


def _whole(shape):
    nd = len(shape)
    return pl.BlockSpec(shape, lambda *_: (0,) * nd)


def _silu(v):
    return v * jax.nn.sigmoid(v)


def _softplus(v):
    return jnp.maximum(v, 0.0) + jnp.log1p(jnp.exp(-jnp.abs(v)))


def _gelu_tanh(v):
    c = math.sqrt(2.0 / math.pi)
    return 0.5 * v * (1.0 + jnp.tanh(c * (v + 0.044715 * (v * v * v))))


def _rms(v, g):
    ms = jnp.mean(v * v, axis=-1, keepdims=True)
    return v * lax.rsqrt(ms + EPS) * g


def _dot(a, b):
    return jnp.dot(a, b, preferred_element_type=F32)


def _dot_nt(a, b):
    return lax.dot_general(a, b, (((1,), (1,)), ((), ())), preferred_element_type=F32)


def _split3(v):
    hi = v.astype(BF16).astype(F32)
    r1 = v - hi
    mid = r1.astype(BF16).astype(F32)
    lo = r1 - mid
    return hi, mid, lo


def _cumsum_rows(tri, v):
    hi, mid, lo = _split3(v)
    return _dot(tri, hi.astype(BF16)) + _dot(tri, mid.astype(BF16)) + _dot(tri, lo.astype(BF16))


def _pack3(v):
    hi, mid, lo = _split3(v)
    lane = lax.broadcasted_iota(jnp.int32, v.shape, 1)
    return jnp.where(lane < SSD_HEADS, hi, jnp.where(lane < 2 * SSD_HEADS, mid, lo)).astype(BF16)


def _spread(v, sel):
    return _dot(_pack3(v), sel)


def _inproj_s5_kernel(x_ref, g_ref, perm_ref, w_ref, zb_ref, ub_ref):
    nb, tt, _ = x_ref.shape
    n = _rms(x_ref[...].reshape(nb * tt, D_MODEL), g_ref[...]).astype(BF16)
    n_tb = _dot(perm_ref[...], n).astype(BF16)
    zb_ref[...] = _dot(n_tb, w_ref[:, :D_B])
    ub_ref[...] = _dot(n_tb, w_ref[:, D_B:])


W_SSD_USED = D_A + CONV_DIM + LANES
W_S5_COLS = 2 * D_B
W_SSD_COLS = 2 * W_S5_COLS


def _w_ssd_spec():
    return pl.BlockSpec((D_MODEL, W_SSD_COLS), lambda *_: (0, 0), pipeline_mode=pl.Buffered(1))


def _w_s5_spec():
    return pl.BlockSpec((D_MODEL, W_S5_COLS), lambda *_: (0, W_SSD_COLS // W_S5_COLS),
                        pipeline_mode=pl.Buffered(1))


def _inproj_s5(x3d, perm, g_in, w_all, tt):
    nb, nt, _ = x3d.shape
    blk2 = pl.BlockSpec((nb * tt, D_B), lambda i: (i, 0))
    return pl.pallas_call(
        _inproj_s5_kernel,
        grid=(nt // tt,),
        in_specs=[pl.BlockSpec((nb, tt, D_MODEL), lambda i: (0, i, 0)),
                  _resident(g_in.shape), _resident(perm.shape), _w_s5_spec()],
        out_specs=[blk2, blk2],
        out_shape=[jax.ShapeDtypeStruct((nb * nt, D_B), F32)] * 2,
        compiler_params=_cparams(("arbitrary",)),
        name="inproj_s5",
    )(x3d, g_in, perm, w_all)


def _inproj_rows_kernel(x_ref, g_ref, ws5_ref, wssd_ref, za_ref, xbc_ref, dt_ref, zb_ref, ub_ref):
    n = _rms(x_ref[...], g_ref[...]).astype(BF16)
    pa = _dot(n, wssd_ref[:, :W_SSD_USED])
    za_ref[...] = pa[:, :D_A]
    xbc_ref[...] = pa[:, D_A:D_A + CONV_DIM]
    dt_ref[...] = pa[:, D_A + CONV_DIM:]
    pb = _dot(n, ws5_ref[...])
    zb_ref[...] = pb[:, :D_B]
    ub_ref[...] = pb[:, D_B:]


def _inproj_rows(x2d, g_in, w_all):
    rows = x2d.shape[0]
    widths = (D_A, CONV_DIM, LANES, D_B, D_B)
    args = (x2d, g_in, w_all, w_all)
    return pl.pallas_call(
        _inproj_rows_kernel,
        grid=(1,),
        in_specs=[_resident(x2d.shape), _resident(g_in.shape), _w_s5_spec(), _w_ssd_spec()],
        out_specs=[_whole((rows, w)) for w in widths],
        out_shape=[jax.ShapeDtypeStruct((rows, w), F32) for w in widths],
        compiler_params=_cparams(("arbitrary",)),
        name="inproj_rows",
    )(*args)


def _ssd_gate_norm(y, za, ng):
    y = y * _silu(za)
    parts = []
    for g in range(SSD_GROUPS):
        sl = slice(g * SSD_GW, (g + 1) * SSD_GW)
        parts.append(_rms(y[:, sl], ng[:, sl]))
    return jnp.concatenate(parts, axis=-1)


def _ssd_chunk_steps(xext, hst, proj, rows, out_ref, cw, cb, dtb, a_neg, dexp, ng, e48, ebc,
                     tri, causal, left):
    L = SSD_CHUNK
    xext[SUBLANES:SUBLANES + L, :] = proj["xbc"][rows]
    ext = xext[...]
    acc = cw[0:1, :] * ext
    for k in range(1, CONV_K):
        acc = cw[k:k + 1, :] * ext + pltpu.roll(acc, 1, axis=0)
        yield
    xc = _silu(acc[SUBLANES:, :] + cb)
    xext[0:SUBLANES, :] = ext[L:L + SUBLANES, :]
    yield

    xs = xc[:, :D_A]
    bm = xc[:, D_A:D_A + SSD_BC]
    cm = xc[:, D_A + SSD_BC:]

    dt = _softplus(proj["dt"][rows] + dtb)
    acs = _cumsum_rows(tri, dt * a_neg)
    ea = jnp.exp(acs)
    wd = jnp.exp(acs[L - 1:L, :] - acs) * dt
    dt_x = _spread(dt, e48)
    wd_x = _spread(wd, e48)
    ea_x = _spread(ea, e48)
    acs_b = _spread(acs, ebc)
    acs_t = acs.T
    yield

    xdt = xs * dt_x
    xw = (xs * wd_x).astype(BF16)
    yield

    ydiag = []
    yoff = []
    for g in range(SSD_GROUPS):
        cg = cm[:, g * SSD_STATE:(g + 1) * SSD_STATE].astype(BF16)
        bg32 = bm[:, g * SSD_STATE:(g + 1) * SSD_STATE]
        bg = bg32.astype(BF16)
        cbm = _dot_nt(cg, bg)
        for j in range(SSD_HPG // 2):
            q = g * (SSD_HPG // 2) + j
            pair = xdt[:, q * LANES:(q + 1) * LANES]
            rhs = [jnp.where(left, pair, 0.0).astype(BF16), jnp.where(left, 0.0, pair).astype(BF16)]
            ms = []
            for i in range(2):
                h = 2 * q + i
                seg = acs_b[:, h * LANES:(h + 1) * LANES] - acs_t[h:h + 1, :]
                dec = jnp.exp(jnp.where(causal, seg, -jnp.inf))
                ms.append((cbm * dec).astype(BF16))
            ydiag.append(_dot(jnp.concatenate(ms, axis=1), jnp.concatenate(rhs, axis=0)))
            yield
        hg = hst[g]
        yoff.append(_dot(cg, hg.astype(BF16)))
        cd = ea_x[L - 1:L, g * SSD_GW:(g + 1) * SSD_GW]
        hst[g] = hg * cd + _dot(bg32.T.astype(BF16), xw[:, g * SSD_GW:(g + 1) * SSD_GW])
        yield

    y = jnp.concatenate(ydiag, axis=1) + jnp.concatenate(yoff, axis=1) * ea_x + dexp * xs
    yield
    out_ref[...] = _ssd_gate_norm(y, proj["za"][rows], ng).astype(out_ref.dtype)


def _project_steps(x_ref, gin, w_ref, proj):
    rows = x_ref.shape[0] * x_ref.shape[1]
    n = _rms(x_ref[...].reshape(rows, D_MODEL), gin).astype(BF16)
    yield
    tile = 2 * LANES
    for name, lo, hi in (("xbc", D_A, D_A + CONV_DIM), ("dt", D_A + CONV_DIM, W_SSD_USED),
                         ("za", 0, D_A)):
        parts = []
        for c0 in range(lo, hi, tile):
            parts.append(_dot(n, w_ref[:, c0:min(c0 + tile, hi)]))
            yield
        proj[name] = parts[0] if len(parts) == 1 else jnp.concatenate(parts, axis=1)


def _interleave(main, filler, ratio):
    k = 0
    for _ in main:
        k += 1
        if filler is not None and k % ratio == 0:
            next(filler, None)
    if filler is not None:
        for _ in filler:
            pass


def _round_robin(*gens):
    live = list(gens)
    while live:
        for g in list(live):
            try:
                next(g)
                yield
            except StopIteration:
                live.remove(g)


def _ssd_prompt_kernel(x_ref, gin_ref, w_ref, cw_ref, cb_ref, dtb_ref, alog_ref,
                       dexp_ref, ng_ref, e48_ref, ebc_ref,
                       ya_ref, hout_ref, cout_ref, xext, hst):
    L = SSD_CHUNK
    ns = x_ref.shape[0]
    c = pl.program_id(1)

    @pl.when(c == 0)
    def _():
        xext[:, 0:SUBLANES, :] = jnp.zeros((ns, SUBLANES, CONV_DIM), F32)
        hst[...] = jnp.zeros(hst.shape, F32)

    row = lax.broadcasted_iota(jnp.int32, (L, L), 0)
    col = lax.broadcasted_iota(jnp.int32, (L, L), 1)
    causal = row >= col
    tri = jnp.where(causal, 1.0, 0.0).astype(BF16)
    left = lax.broadcasted_iota(jnp.int32, (L, LANES), 1) < SSD_HEAD_DIM
    a_neg = -jnp.exp(alog_ref[...])

    gsz = SSD_SEQS_PER_MATMUL
    groups = list(range(0, ns, gsz))
    projs = [dict() for _ in groups]
    project = lambda k: _project_steps(x_ref.at[groups[k]:groups[k] + gsz], gin_ref[...], w_ref, projs[k])
    _interleave(project(0), None, 1)
    for k, s0 in enumerate(groups):
        scans = _round_robin(*[
            _ssd_chunk_steps(xext.at[s0 + i], hst.at[s0 + i], projs[k], slice(i * L, (i + 1) * L),
                             ya_ref.at[s0 + i], cw_ref[...], cb_ref[...], dtb_ref[...], a_neg,
                             dexp_ref[...], ng_ref[...], e48_ref[...], ebc_ref[...], tri, causal, left)
            for i in range(gsz)])
        _interleave(scans, project(k + 1) if k + 1 < len(groups) else None, SSD_SCAN_STEPS_PER_PROJ_STEP)

    @pl.when(c == pl.num_programs(1) - 1)
    def _():
        for s in range(ns):
            for g in range(SSD_GROUPS):
                hout_ref[s, g * SSD_GW:(g + 1) * SSD_GW, :] = hst[s, g].T
            cout_ref[s] = xext[s, SUBLANES - (CONV_K - 1):SUBLANES, :]


def _ssd_prompt(x3d, g_in, w_all, cw, cb, dtb, alog, dexp, ng, e48, ebc):
    nb, nt, _ = x3d.shape
    L = SSD_CHUNK
    ns = min(nb, SSD_SEQS_PER_STEP)
    blk = lambda w: pl.BlockSpec((ns, L, w), lambda b, c: (b, c, 0))
    per_seq = lambda r, w: pl.BlockSpec((ns, r, w), lambda b, c: (b, 0, 0))
    consts = (cw, cb, dtb, alog, dexp, ng, e48, ebc)
    return pl.pallas_call(
        _ssd_prompt_kernel,
        grid=(nb // ns, nt // L),
        in_specs=[blk(D_MODEL), _resident(g_in.shape), _w_ssd_spec()]
        + [_resident(a.shape) for a in consts],
        out_specs=[blk(D_A), per_seq(D_A, SSD_STATE), per_seq(CONV_K - 1, CONV_DIM)],
        out_shape=[
            jax.ShapeDtypeStruct((nb, nt, D_A), BF16),
            jax.ShapeDtypeStruct((nb, D_A, SSD_STATE), F32),
            jax.ShapeDtypeStruct((nb, CONV_K - 1, CONV_DIM), F32),
        ],
        scratch_shapes=[
            pltpu.VMEM((ns, L + SUBLANES, CONV_DIM), F32),
            pltpu.VMEM((ns, SSD_GROUPS, SSD_STATE, SSD_GW), F32),
        ],
        compiler_params=_cparams(("arbitrary", "arbitrary")),
        name="ssd_prompt",
    )(x3d, g_in, w_all, *consts)


def _ssd_step_pre_kernel(nb, nt, xbc_ref, cbuf_ref, dt_ref, cw_ref, cb_ref, dtb_ref, alog_ref,
                         dexp_ref, e48_ref,
                         ypart_ref, cm_ref, bm_ref, xw_ref, eax_ref, cd_ref):
    full = [cbuf_ref[k * nb:(k + 1) * nb, :] for k in range(CONV_K - 1)]
    full += [xbc_ref[t * nb:(t + 1) * nb, :] for t in range(nt)]
    e48 = e48_ref[...]
    a_neg = -jnp.exp(alog_ref[...])
    lane = lax.broadcasted_iota(jnp.int32, (nb, LANES), 1)
    first_group = (lane % SSD_HEADS) < SSD_HPG

    xs, bm, cm, dts, acs = [], [], [], [], []
    run = None
    for t in range(nt):
        acc = cb_ref[...] + cw_ref[0:1, :] * full[t]
        for k in range(1, CONV_K):
            acc = acc + cw_ref[k:k + 1, :] * full[t + k]
        xc = _silu(acc)
        xs.append(xc[:, :D_A])
        bm.append(xc[:, D_A:D_A + SSD_BC])
        cm.append(xc[:, D_A + SSD_BC:])
        d = _softplus(dt_ref[t * nb:(t + 1) * nb, :] + dtb_ref[...])
        dts.append(d)
        run = d * a_neg if run is None else run + d * a_neg
        acs.append(run)

    for l in range(nt):
        y = dexp_ref[...] * xs[l]
        for s in range(l + 1):
            cbv = []
            for g in range(SSD_GROUPS):
                sl = slice(g * SSD_STATE, (g + 1) * SSD_STATE)
                cbv.append(jnp.sum(cm[l][:, sl] * bm[s][:, sl], axis=-1, keepdims=True))
            coef = jnp.where(first_group, cbv[0], cbv[1]) * jnp.exp(acs[l] - acs[s]) * dts[s]
            y = y + _spread(coef, e48) * xs[s]
        rows = slice(l * nb, (l + 1) * nb)
        ypart_ref[rows, :] = y
        cm_ref[rows, :] = cm[l]
        bm_ref[rows, :] = bm[l]
        wd = jnp.exp(acs[nt - 1] - acs[l]) * dts[l]
        xw_ref[rows, :] = xs[l] * _spread(wd, e48)
        eax_ref[rows, :] = _spread(jnp.exp(acs[l]), e48)
    cd_ref[...] = jnp.exp(acs[nt - 1])


def _ssd_step_pre(xbc, cbuf, dt, nb, nt, cw, cb, dtb, alog, dexp, e48):
    rows = nb * nt
    args = (xbc, cbuf, dt, cw, cb, dtb, alog, dexp, e48)
    widths = (D_A, SSD_BC, SSD_BC, D_A, D_A)
    return pl.pallas_call(
        functools.partial(_ssd_step_pre_kernel, nb, nt),
        grid=(1,),
        in_specs=[_resident(a.shape) for a in args],
        out_specs=[_whole((rows, w)) for w in widths] + [_whole((nb, LANES))],
        out_shape=[jax.ShapeDtypeStruct((rows, w), F32) for w in widths]
        + [jax.ShapeDtypeStruct((nb, LANES), F32)],
        compiler_params=_cparams(("arbitrary",)),
        name="ssd_step_pre",
    )(*args)


def _ssd_step_state_kernel(nt, ns, cd_ref, cm_ref, bm_ref, xw_ref, eax_ref, ypart_ref, za_ref, ng_ref,
                           h0_ref, ya_ref, hnew_ref):
    b0 = pl.program_id(0) * ns
    for s in range(ns):
        yoff = []
        for g in range(SSD_GROUPS):
            cg = cm_ref[:, s, g * SSD_STATE:(g + 1) * SSD_STATE].astype(BF16)
            bg = bm_ref[:, s, g * SSD_STATE:(g + 1) * SSD_STATE].astype(BF16)
            xwg = xw_ref[:, s, g * SSD_GW:(g + 1) * SSD_GW].astype(BF16)
            hg = h0_ref[s, g * SSD_GW:(g + 1) * SSD_GW, :]
            yoff.append(lax.dot_general(cg, hg.astype(BF16), (((1,), (1,)), ((), ())),
                                        preferred_element_type=F32))
            upd = lax.dot_general(xwg, bg, (((0,), (0,)), ((), ())),
                                  preferred_element_type=F32)
            for e in range(SSD_HPG):
                r0 = g * SSD_GW + e * SSD_HEAD_DIM
                decay = cd_ref[b0 + s, g * SSD_HPG + e]
                hnew_ref[s, r0:r0 + SSD_HEAD_DIM, :] = (
                    h0_ref[s, r0:r0 + SSD_HEAD_DIM, :] * decay
                    + upd[e * SSD_HEAD_DIM:(e + 1) * SSD_HEAD_DIM, :])
        y = ypart_ref[:, s, :] + jnp.concatenate(yoff, axis=1) * eax_ref[:, s, :]
        ya_ref[:, s, :] = _ssd_gate_norm(y, za_ref[:, s, :], ng_ref[...])


def _ssd_step_state(cd, cm, bm, xw, eax, ypart, za, ng, h0, nb, nt):
    ns = STATE_SEQS_PER_STEP
    blk = lambda w: pl.BlockSpec((nt, ns, w), lambda b: (0, b, 0))
    state = pl.BlockSpec((ns, D_A, SSD_STATE), lambda b: (b, 0, 0))
    return pl.pallas_call(
        functools.partial(_ssd_step_state_kernel, nt, ns),
        grid=(nb // ns,),
        in_specs=[
            pl.BlockSpec(memory_space=pltpu.SMEM),
            blk(SSD_BC), blk(SSD_BC), blk(D_A), blk(D_A), blk(D_A), blk(D_A),
            _resident(ng.shape), state,
        ],
        out_specs=[blk(D_A), state],
        out_shape=[
            jax.ShapeDtypeStruct((nt, nb, D_A), F32),
            jax.ShapeDtypeStruct((nb, D_A, SSD_STATE), F32),
        ],
        compiler_params=_cparams(("arbitrary",)),
        name="ssd_step_state",
    )(cd, cm, bm, xw, eax, ypart, za, ng, h0)


def _s5_block_diagonals(bre_ref, bim_ref, cre_ref, cim_ref, bs_ref, cs_ref):
    gs = S5_GROUPS // S5_SLABS
    nst, nch = gs * S5_STATE, gs * S5_CH
    iota = lambda shape, d: lax.broadcasted_iota(jnp.int32, shape, d)
    rep_ch = jnp.where(iota((S5_CH, nch), 0) == iota((S5_CH, nch), 1) % S5_CH, 1.0, 0.0).astype(BF16)
    rep_st = jnp.where(iota((S5_STATE, nst), 0) == iota((S5_STATE, nst), 1) % S5_STATE,
                       1.0, 0.0).astype(BF16)
    own_b = iota((nst, nch), 0) // S5_STATE == iota((nst, nch), 1) // S5_CH
    own_c = iota((nch, nst), 0) // S5_CH == iota((nch, nst), 1) // S5_STATE
    for s in range(S5_SLABS):
        for part, src in enumerate((bre_ref, bim_ref)):
            t = _dot(src[s * nst:(s + 1) * nst, :].astype(BF16), rep_ch)
            bs_ref[s, part * nst:(part + 1) * nst, :] = jnp.where(own_b, t, 0.0).astype(BF16)
        for part, (src, sign) in enumerate(((cre_ref, 1.0), (cim_ref, -1.0))):
            t = _dot(src[s * nch:(s + 1) * nch, :].astype(BF16), rep_st)
            cs_ref[s, :, part * nst:(part + 1) * nst] = jnp.where(own_c, sign * t, 0.0).astype(BF16)


def _s5_kernel(nb, ts, u_ref, zb_ref, h0re_ref, h0im_ref, are_ref, aim_ref, bre_ref, bim_ref,
               cre_ref, cim_ref, d_ref, gluw_ref, glub_ref,
               yb_ref, hre_ref, him_ref, bu, sre, sim, yacc, bs_ref, cs_ref):
    i = pl.program_id(0)

    @pl.when(i == 0)
    def _():
        sre[...] = h0re_ref[...]
        sim[...] = h0im_ref[...]
        _s5_block_diagonals(bre_ref, bim_ref, cre_ref, cim_ref, bs_ref, cs_ref)

    u = u_ref[...]
    ub = u.astype(BF16)
    st = S5_SLAB_ST
    for s in range(S5_SLABS):
        cin = slice(s * S5_SLAB_IN, (s + 1) * S5_SLAB_IN)
        cst = slice(s * st, (s + 1) * st)
        buf = bu.at[s % 2]
        buf[...] = _dot_nt(ub[:, cin], bs_ref[s])
        ar = jnp.broadcast_to(are_ref[:, cst], (SUBLANES, st))
        ai = jnp.broadcast_to(aim_ref[:, cst], (SUBLANES, st))
        for bg in range(nb // SUBLANES):
            rb = slice(bg * SUBLANES, (bg + 1) * SUBLANES)
            xr = sre[rb, cst]
            xi = sim[rb, cst]
            for t in range(ts):
                rt = slice(t * nb + bg * SUBLANES, t * nb + (bg + 1) * SUBLANES)
                nxr = ar * xr - ai * xi + buf[rt, 0:st]
                nxi = ar * xi + ai * xr + buf[rt, st:2 * st]
                buf[rt, 0:st] = nxr
                buf[rt, st:2 * st] = nxi
                xr, xi = nxr, nxi
            sre[rb, cst] = xr
            sim[rb, cst] = xi
        yacc[:, cin] = _dot_nt(buf[...].astype(BF16), cs_ref[s]) + d_ref[:, cin] * u[:, cin]

    g = _gelu_tanh(yacc[...])
    gate = jax.nn.sigmoid(_dot(g.astype(BF16), gluw_ref[...]) + glub_ref[...])
    yb_ref[...] = (g * gate * _silu(zb_ref[...])).astype(BF16)

    @pl.when(i == pl.num_programs(0) - 1)
    def _():
        hre_ref[...] = sre[...]
        him_ref[...] = sim[...]


def _s5(u, zb, h0re, h0im, a_re, a_im, b_re, b_im, c_re, c_im, dvec, gluw, glub, nb, nt, ts):
    rows = nb * ts
    blk = pl.BlockSpec((rows, D_B), lambda i: (i, 0))
    consts = (h0re, h0im, a_re, a_im, b_re, b_im, c_re, c_im, dvec, gluw, glub)
    return pl.pallas_call(
        functools.partial(_s5_kernel, nb, ts),
        grid=(nt // ts,),
        in_specs=[blk, blk] + [_resident(a.shape) for a in consts],
        out_specs=[blk, _whole((nb, S5_LANES)), _whole((nb, S5_LANES))],
        out_shape=[
            jax.ShapeDtypeStruct((nb * nt, D_B), BF16),
            jax.ShapeDtypeStruct((nb, S5_LANES), F32),
            jax.ShapeDtypeStruct((nb, S5_LANES), F32),
        ],
        scratch_shapes=[
            pltpu.VMEM((2, rows, 2 * S5_SLAB_ST), F32),
            pltpu.VMEM((nb, S5_LANES), F32),
            pltpu.VMEM((nb, S5_LANES), F32),
            pltpu.VMEM((rows, D_B), F32),
            pltpu.VMEM((S5_SLABS, 2 * S5_SLAB_ST, S5_SLAB_IN), BF16),
            pltpu.VMEM((S5_SLABS, S5_SLAB_IN, 2 * S5_SLAB_ST), BF16),
        ],
        compiler_params=_cparams(("arbitrary",)),
        name="s5",
    )(u, zb, *consts)


def _outproj_kernel(permute, *refs):
    if permute:
        x_ref, ya_ref, yb_ref, p_ref, perm_ref = refs[:5]
    else:
        x_ref, ya_ref, yb_ref, p_ref = refs[:4]
    woa_ref, wob_ref, gple_ref, wg_ref, wp_ref, gfin_ref, y_ref = refs[-7:]
    nb, tt, _ = x_ref.shape
    rows = nb * tt
    yb = yb_ref[...]
    if permute:
        yb = _dot(perm_ref[...], yb).astype(BF16)
    ya = ya_ref[...].reshape(rows, D_A).astype(BF16)
    h = x_ref[...].reshape(rows, D_MODEL) + _dot(ya, woa_ref[...]) + _dot(yb, wob_ref[...])
    gate = jax.nn.sigmoid(_dot(_rms(h, gple_ref[...]).astype(BF16), wg_ref[...]))
    h = h + _dot(p_ref[...].reshape(rows, PLE_DIM).astype(BF16), wp_ref[...]) * gate
    y_ref[...] = _rms(h, gfin_ref[...]).reshape(nb, tt, D_MODEL)


def _outproj(x3d, ya3d, yb2d, p3d, perm_t, woa, wob, gple, wg, wp, gfin, tt):
    nb, nt, _ = x3d.shape
    rows = nb * tt
    blk3 = lambda w: pl.BlockSpec((nb, tt, w), lambda i: (0, i, 0))
    consts = (woa, wob, gple, wg, wp, gfin)
    args = (x3d, ya3d, yb2d, p3d) + ((perm_t,) if perm_t is not None else ()) + consts
    in_specs = [blk3(D_MODEL), blk3(D_A), pl.BlockSpec((rows, D_B), lambda i: (i, 0)), blk3(PLE_DIM)]
    in_specs += [_resident(perm_t.shape)] if perm_t is not None else []
    in_specs += [_resident(a.shape) for a in consts]
    return pl.pallas_call(
        functools.partial(_outproj_kernel, perm_t is not None),
        grid=(nt // tt,),
        in_specs=in_specs,
        out_specs=blk3(D_MODEL),
        out_shape=jax.ShapeDtypeStruct((nb, nt, D_MODEL), F32),
        compiler_params=_cparams(("arbitrary",)),
        name="outproj",
    )(*args)


def _head_lanes(v):
    rep = jnp.tile(v, (1, HEAD_REPLICAS))
    return jnp.pad(rep, ((0, 0), (0, LANES - rep.shape[-1])))


def _time_major_perm(nb, tt):
    r = jnp.arange(nb * tt)
    src = (r % nb) * tt + r // nb
    return (src[:, None] == jnp.arange(nb * tt)[None, :]).astype(BF16)


def _s5_params(lam_re, lam_im, log_dt, b_re, b_im, c_re, c_im):
    lr = lam_re.astype(F32)
    li = lam_im.astype(F32)
    delta = jnp.exp(log_dt.astype(F32))[:, None]
    mag = jnp.exp(lr * delta)
    ab_re = mag * jnp.cos(li * delta)
    ab_im = mag * jnp.sin(li * delta)
    q_re = ab_re - 1.0
    den = lr * lr + li * li
    f_re = (q_re * lr + ab_im * li) / den
    f_im = (ab_im * lr - q_re * li) / den
    bb_re = f_re[..., None] * b_re.astype(F32) - f_im[..., None] * b_im.astype(F32)
    bb_im = f_re[..., None] * b_im.astype(F32) + f_im[..., None] * b_re.astype(F32)
    return (ab_re.reshape(1, S5_LANES), ab_im.reshape(1, S5_LANES),
            bb_re.reshape(S5_LANES, S5_CH), bb_im.reshape(S5_LANES, S5_CH),
            c_re.astype(F32).reshape(D_B, S5_STATE), c_im.astype(F32).reshape(D_B, S5_STATE))


def _layer(i, x_prompt, x_sample_tb, p_prompt, p_sample_tb, state_ssd, state_conv, state_s5_re,
           state_s5_im, w_in, g_in, conv_w, conv_b, dt_bias, a_log, ssd_d, ssd_norm_g,
           s5_lambda_re, s5_lambda_im, s5_log_dt, s5_b_re, s5_b_im, s5_c_re, s5_c_im, s5_d,
           glu_w, glu_b, w_out, g_ple, w_ple_gate, w_ple_proj, g_final):
    bp, tp, _ = x_prompt.shape
    ts_, bs_, _ = x_sample_tb.shape

    i1 = D_A
    i2 = i1 + CONV_DIM
    i3 = i2 + SSD_HEADS
    i4 = i3 + D_B
    w = w_in[i]
    pad = jnp.zeros((D_MODEL, W_SSD_COLS - i2 - HEAD_REPLICAS * SSD_HEADS), F32)
    w_all = jnp.concatenate([w[:, :i2]] + [w[:, i2:i3]] * HEAD_REPLICAS + [pad, w[:, i3:]],
                            axis=1).astype(BF16)
    gin = g_in[i].reshape(1, D_MODEL)
    cw = conv_w[i]
    cb = conv_b[i].reshape(1, CONV_DIM)
    dtb = _head_lanes(dt_bias[i].reshape(1, SSD_HEADS))
    alog = _head_lanes(a_log[i].reshape(1, SSD_HEADS))
    dexp = jnp.repeat(ssd_d[i], SSD_HEAD_DIM).reshape(1, D_A)
    ng = ssd_norm_g[i].reshape(1, D_A)
    src = jnp.arange(LANES)
    src_head = jnp.where(src < HEAD_REPLICAS * SSD_HEADS, src % SSD_HEADS, -1)[:, None]
    e48 = (src_head == (jnp.arange(D_A) // SSD_HEAD_DIM)[None, :]).astype(BF16)
    ebc = (src_head == (jnp.arange(SSD_HEADS * LANES) // LANES)[None, :]).astype(BF16)
    s5p = _s5_params(s5_lambda_re[i], s5_lambda_im[i], s5_log_dt[i],
                     s5_b_re[i], s5_b_im[i], s5_c_re[i], s5_c_im[i])
    dvec = s5_d[i].reshape(1, D_B)
    gluw = glu_w[i].astype(BF16)
    glub = glu_b[i].reshape(1, D_B)
    wo = w_out[i].astype(BF16)
    woa, wob = wo[:D_A], wo[D_A:]
    gple = g_ple[i].reshape(1, D_MODEL)
    wg = w_ple_gate[i].astype(BF16)
    wp = w_ple_proj[i].astype(BF16)
    gfin = g_final.reshape(1, D_MODEL)

    tt = min(tp, TIME_TILE)
    perm = _time_major_perm(bp, tt)
    zb, ub = _inproj_s5(x_prompt, perm, gin, w_all, tt)
    ya, ssd_p, conv_p = _ssd_prompt(x_prompt, gin, w_all, cw, cb, dtb, alog, dexp, ng, e48, ebc)
    zeros_s5 = jnp.zeros((bp, S5_LANES), F32)
    yb, re_p, im_p = _s5(ub, zb, zeros_s5, zeros_s5, *s5p, dvec, gluw, glub, bp, tp, tt)
    hp = _outproj(x_prompt, ya, yb, p_prompt[i], perm.T, woa, wob, gple, wg, wp, gfin, tt)
    out_p = (hp, ssd_p.reshape(bp, SSD_HEADS, SSD_HEAD_DIM, SSD_STATE), conv_p,
             re_p.reshape(bp, S5_GROUPS, S5_STATE), im_p.reshape(bp, S5_GROUPS, S5_STATE))

    rows = ts_ * bs_
    xs3d = x_sample_tb.reshape(1, rows, D_MODEL)
    za, xbc, dt, zb, ub = _inproj_rows(xs3d[0], gin, w_all)
    cbuf = jnp.transpose(state_conv[i], (1, 0, 2)).reshape((CONV_K - 1) * bs_, CONV_DIM)
    ypart, cm, bm, xw, eax, cd = _ssd_step_pre(xbc, cbuf, dt, bs_, ts_, cw, cb, dtb, alog, dexp, e48)
    v2 = lambda a: a.reshape(ts_, bs_, a.shape[-1])
    ya, ssd_s = _ssd_step_state(cd[:, :SSD_HEADS], v2(cm), v2(bm), v2(xw), v2(eax), v2(ypart), v2(za),
                                ng, state_ssd[i].reshape(bs_, D_A, SSD_STATE), bs_, ts_)
    yb, re_s, im_s = _s5(ub, zb, state_s5_re[i].reshape(bs_, S5_LANES),
                         state_s5_im[i].reshape(bs_, S5_LANES),
                         *s5p, dvec, gluw, glub, bs_, ts_, ts_)
    hs = _outproj(xs3d, ya.reshape(1, rows, D_A), yb, p_sample_tb[i].reshape(1, rows, PLE_DIM), None,
                  woa, wob, gple, wg, wp, gfin, rows)
    hist = jnp.concatenate([cbuf.reshape(CONV_K - 1, bs_, CONV_DIM), xbc.reshape(ts_, bs_, CONV_DIM)], axis=0)
    conv_s = jnp.transpose(hist[-(CONV_K - 1):], (1, 0, 2))
    out_s = (hs.reshape(ts_, bs_, D_MODEL),
             ssd_s.reshape(bs_, SSD_HEADS, SSD_HEAD_DIM, SSD_STATE), conv_s,
             re_s.reshape(bs_, S5_GROUPS, S5_STATE), im_s.reshape(bs_, S5_GROUPS, S5_STATE))
    return out_p, out_s


def kernel(x_prompt, x_sample, p_prompt, p_sample, state_ssd, state_conv, state_s5_re, state_s5_im, w_in, g_in, conv_w, conv_b, dt_bias, a_log, ssd_d, ssd_norm_g, s5_lambda_re, s5_lambda_im, s5_log_dt, s5_b_re, s5_b_im, s5_c_re, s5_c_im, s5_d, glu_w, glu_b, w_out, g_ple, w_ple_gate, w_ple_proj, g_final):
    depth = w_in.shape[0]
    assert depth == 1, "the final RMSNorm is fused into the single layer's epilogue"
    xs_tb = jnp.transpose(x_sample, (1, 0, 2))
    ps_tb = jnp.transpose(p_sample, (0, 2, 1, 3))
    out_p, out_s = _layer(0, x_prompt, xs_tb, p_prompt, ps_tb, state_ssd, state_conv, state_s5_re,
                          state_s5_im, w_in, g_in, conv_w, conv_b, dt_bias, a_log, ssd_d, ssd_norm_g,
                          s5_lambda_re, s5_lambda_im, s5_log_dt, s5_b_re, s5_b_im, s5_c_re, s5_c_im,
                          s5_d, glu_w, glu_b, w_out, g_ple, w_ple_gate, w_ple_proj, g_final)
    y_sample = jnp.transpose(out_s[0], (1, 0, 2))
    stack = lambda a: a[None]
    return (out_p[0], y_sample,
            stack(out_p[1]), stack(out_p[2]), stack(out_p[3]), stack(out_p[4]),
            stack(out_s[1]), stack(out_s[2]), stack(out_s[3]), stack(out_s[4]))
```
